```python
import math
import jax, jax.numpy as jnp
from jax import lax
import numpy as np

D_MODEL = 4096
BATCH = 4
SEQ = 2048
DEPTH = 1

ATTN_HEADS = 16
ATTN_HEAD_DIM = 64
ATTN_V_DIM = 2 * ATTN_HEAD_DIM
D_ATTN_QK = ATTN_HEADS * 2 * ATTN_HEAD_DIM
D_ATTN_V = ATTN_HEADS * ATTN_V_DIM
Q_BLOCK = 128
ROPE_THETA = 10000.0

SSM_HEAD_DIM = 64
D_SSM = D_MODEL // 2
SSM_HEADS = D_SSM // SSM_HEAD_DIM
SSM_GROUPS = 8
SSM_STATE = 128
SSM_CHUNK = 128
CONV_WIDTH = 5
D_CONV_CH = D_SSM + 2 * SSM_GROUPS * SSM_STATE

GATE_W = 2 * D_MODEL
IN_WIDTH = 2 * D_ATTN_QK + D_ATTN_V + D_SSM + D_CONV_CH + 2 * SSM_HEADS + GATE_W

N_EXPERTS = 16
EXPERT_FF = 2048
CAPACITY_FACTOR = 2

ALPHA = (2.0 * DEPTH) ** 0.25
BETA = (8.0 * DEPTH) ** -0.25

kernel_name = 'hybrid_diffattn_mamba2_ecmoe_encoder'


def layer_norm(x, g, b, eps=1e-5):
    xf = x.astype(jnp.float32)
    mu = jnp.mean(xf, axis=-1, keepdims=True)
    var = jnp.mean(jnp.square(xf - mu), axis=-1, keepdims=True)
    return ((xf - mu) * lax.rsqrt(var + eps) * g.astype(jnp.float32) + b.astype(jnp.float32)).astype(x.dtype)


def rms_norm(x, w, eps=1e-6):
    xf = x.astype(jnp.float32)
    y = xf * lax.rsqrt(jnp.mean(xf * xf, axis=-1, keepdims=True) + eps)
    return (y * w.astype(jnp.float32)).astype(x.dtype)


def rope_tables(seq, dim):
    inv_freq = ROPE_THETA ** (-jnp.arange(0, dim, 2, dtype=jnp.float32) / dim)
    ang = jnp.arange(seq, dtype=jnp.float32)[:, None] * inv_freq[None, :]
    return jnp.cos(ang), jnp.sin(ang)


def apply_rope(x, cos, sin):
    shape = (1, x.shape[1]) + (1,) * (x.ndim - 3) + (cos.shape[-1],)
    c = cos.reshape(shape)
    s = sin.reshape(shape)
    xf = x.astype(jnp.float32)
    x1, x2 = jnp.split(xf, 2, axis=-1)
    return jnp.concatenate([x1 * c - x2 * s, x2 * c + x1 * s], axis=-1).astype(x.dtype)


def diff_attention(q, k, v, lam, subln_w, lambda_init):
    bsz, seq = q.shape[0], q.shape[1]
    cos, sin = rope_tables(seq, ATTN_HEAD_DIM)
    q = apply_rope(q, cos, sin) * (ATTN_HEAD_DIM ** -0.5)
    k = apply_rope(k, cos, sin)
    n_blk = seq // Q_BLOCK
    qb = jnp.moveaxis(q.reshape(bsz, n_blk, Q_BLOCK, ATTN_HEADS, 2, ATTN_HEAD_DIM), 1, 0)

    def block(q_blk):
        s = jnp.einsum('bqhcd,bkhcd->bhcqk', q_blk, k).astype(jnp.float32)
        p = jax.nn.softmax(s, axis=-1)
        p = p[:, :, 0] - lam * p[:, :, 1]
        return jnp.einsum('bhqk,bkhe->bqhe', p.astype(v.dtype), v)

    o = lax.map(block, qb)
    o = jnp.moveaxis(o, 0, 1).reshape(bsz, seq, ATTN_HEADS, ATTN_V_DIM)
    o = rms_norm(o, subln_w) * (1.0 - lambda_init)
    return o.reshape(bsz, seq, D_ATTN_V)


def centred_depthwise_conv(x, w, b):
    pad = (CONV_WIDTH - 1) // 2
    y = lax.conv_general_dilated(x, w[:, None, :], window_strides=(1,), padding=[(pad, pad)],
                                 dimension_numbers=('NWC', 'WIO', 'NWC'),
                                 feature_group_count=x.shape[-1])
    return y + b


def ssd_scan(xs, dt, a, bm, cm):
    bsz, seq, n_h, p_dim = xs.shape
    n_g, n_s = bm.shape[2], bm.shape[3]
    r = n_h // n_g
    n_c, lc = seq // SSM_CHUNK, SSM_CHUNK
    dt = dt.astype(jnp.float32)
    x = (xs.astype(jnp.float32) * dt[..., None]).reshape(bsz, n_c, lc, n_g, r, p_dim)
    la = (dt * a.astype(jnp.float32)).reshape(bsz, n_c, lc, n_g, r)
    bm = bm.astype(jnp.float32).reshape(bsz, n_c, lc, n_g, n_s)
    cm = cm.astype(jnp.float32).reshape(bsz, n_c, lc, n_g, n_s)
    cs = jnp.cumsum(la, axis=2)
    diff = cs[:, :, :, None] - cs[:, :, None, :]
    lower = jnp.tril(jnp.ones((lc, lc), dtype=bool))[None, None, :, :, None, None]
    decay = jnp.exp(jnp.where(lower, diff, -jnp.inf))
    cb = jnp.einsum('bclgn,bcsgn->bclsg', cm, bm)
    y_diag = jnp.einsum('bclsg,bclsgr,bcsgrp->bclgrp', cb, decay, x)
    decay_to_end = jnp.exp(cs[:, :, -1:] - cs)
    states = jnp.einsum('bclgn,bclgr,bclgrp->bcgrpn', bm, decay_to_end, x)
    chunk_decay = jnp.exp(cs[:, :, -1])

    def step(h, inp):
        st, dec = inp
        return h * dec[..., None, None] + st, h

    h0 = jnp.zeros((bsz, n_g, r, p_dim, n_s), jnp.float32)
    _, h_in = lax.scan(step, h0, (jnp.moveaxis(states, 1, 0), jnp.moveaxis(chunk_decay, 1, 0)))
    h_in = jnp.moveaxis(h_in, 0, 1)
    y_off = jnp.einsum('bclgn,bcgrpn,bclgr->bclgrp', cm, h_in, jnp.exp(cs))
    return (y_diag + y_off).reshape(bsz, seq, n_h, p_dim).astype(xs.dtype)


def mamba2_bidirectional(z, xbc, dt_raw, conv_w, conv_b, dt_bias_fwd, dt_bias_bwd,
                         a_log_fwd, a_log_bwd, d_skip, norm_w):
    bsz, seq = z.shape[0], z.shape[1]
    xbc = jax.nn.silu(centred_depthwise_conv(xbc, conv_w, conv_b))
    gn = SSM_GROUPS * SSM_STATE
    xs = xbc[..., :D_SSM].reshape(bsz, seq, SSM_HEADS, SSM_HEAD_DIM)
    bm = xbc[..., D_SSM:D_SSM + gn].reshape(bsz, seq, SSM_GROUPS, SSM_STATE)
    cm = xbc[..., D_SSM + gn:].reshape(bsz, seq, SSM_GROUPS, SSM_STATE)
    dt_f = jax.nn.softplus(dt_raw[..., :SSM_HEADS].astype(jnp.float32) + dt_bias_fwd.astype(jnp.float32))
    dt_b = jax.nn.softplus(dt_raw[..., SSM_HEADS:].astype(jnp.float32) + dt_bias_bwd.astype(jnp.float32))
    a_f = -jnp.exp(a_log_fwd.astype(jnp.float32))
    a_b = -jnp.exp(a_log_bwd.astype(jnp.float32))
    flip = lambda t: t[:, ::-1]
    y_f = ssd_scan(xs, dt_f, a_f, bm, cm)
    y_b = flip(ssd_scan(flip(xs), flip(dt_b), a_b, flip(bm), flip(cm)))
    y = y_f + y_b + xs * d_skip[:, None]
    y = y.reshape(bsz, seq, D_SSM) * jax.nn.silu(z)
    y = rms_norm(y.reshape(bsz, seq, SSM_GROUPS, D_SSM // SSM_GROUPS),
                 norm_w.reshape(SSM_GROUPS, D_SSM // SSM_GROUPS))
    return y.reshape(bsz, seq, D_SSM)


def expert_choice_ffn(x, w_router, w_gate_e, w_up_e, w_down_e):
    bsz, seq, d = x.shape
    cap = CAPACITY_FACTOR * seq // N_EXPERTS
    logits = jnp.einsum('bsd,de->bes', x, w_router).astype(jnp.float32)
    aff = jax.nn.softmax(logits, axis=1)
    g, idx = lax.top_k(aff, cap)
    xg = jax.vmap(lambda xb, ib: xb[ib])(x, idx)
    h = jax.nn.silu(jnp.einsum('becd,edf->becf', xg, w_gate_e)) * jnp.einsum('becd,edf->becf', xg, w_up_e)
    y = jnp.einsum('becf,efd->becd', h, w_down_e) * g[..., None].astype(x.dtype)
    flat = (jnp.arange(bsz, dtype=jnp.int32)[:, None, None] * seq + idx).reshape(-1)
    out = jnp.zeros((bsz * seq, d), y.dtype).at[flat].add(y.reshape(-1, d))
    return out.reshape(bsz, seq, d)


def hybrid_layer(x, w_in, b_gate, lambda_q1, lambda_k1, lambda_q2, lambda_k2, attn_subln_w,
                 conv_w, conv_b, dt_bias_fwd, dt_bias_bwd, a_log_fwd, a_log_bwd, d_skip, ssm_norm_w,
                 w_branch, w_out, ln1_g, ln1_b, w_router, w_gate_e, w_up_e, w_down_e, ln2_g, ln2_b,
                 layer_idx):
    bsz, seq, _ = x.shape
    proj = jnp.einsum('bsd,dk->bsk', x, w_in)
    sizes = [D_ATTN_QK, D_ATTN_QK, D_ATTN_V, D_SSM, D_CONV_CH, 2 * SSM_HEADS, GATE_W]
    cuts = list(np.cumsum(sizes)[:-1])
    q, k, v, z, xbc, dt_raw, gate_pre = jnp.split(proj, cuts, axis=-1)
    lambda_init = 0.8 - 0.6 * math.exp(-0.3 * layer_idx)
    f32 = jnp.float32
    lam = (jnp.exp(jnp.sum(lambda_q1.astype(f32) * lambda_k1.astype(f32)))
           - jnp.exp(jnp.sum(lambda_q2.astype(f32) * lambda_k2.astype(f32))) + lambda_init)
    q = q.reshape(bsz, seq, ATTN_HEADS, 2, ATTN_HEAD_DIM)
    k = k.reshape(bsz, seq, ATTN_HEADS, 2, ATTN_HEAD_DIM)
    v = v.reshape(bsz, seq, ATTN_HEADS, ATTN_V_DIM)
    y_attn = diff_attention(q, k, v, lam, attn_subln_w, lambda_init)
    y_ssm = mamba2_bidirectional(z, xbc, dt_raw, conv_w, conv_b, dt_bias_fwd, dt_bias_bwd,
                                 a_log_fwd, a_log_bwd, d_skip, ssm_norm_w)
    gates = jax.nn.sigmoid(gate_pre + b_gate)
    g_attn, g_ssm = gates[..., :D_MODEL], gates[..., D_MODEL:]
    br_attn = jnp.einsum('bsk,kd->bsd', y_attn, w_branch[:D_ATTN_V])
    br_ssm = jnp.einsum('bsk,kd->bsd', y_ssm, w_branch[D_ATTN_V:])
    mixed = jnp.einsum('bsk,kd->bsd', g_attn * br_attn + g_ssm * br_ssm, w_out)
    x = layer_norm(ALPHA * x + mixed, ln1_g, ln1_b)
    x = layer_norm(ALPHA * x + expert_choice_ffn(x, w_router, w_gate_e, w_up_e, w_down_e), ln2_g, ln2_b)
    return x


def setup_inputs(seed: int = 0) -> dict:
    key = jax.random.key(seed)
    ks = jax.random.split(key, 28)
    f32 = jnp.float32
    L = DEPTH

    def nrm(k, shape, fan_in, gain=1.0):
        return jax.random.normal(k, shape, f32) * (gain * fan_in ** -0.5)

    x = jax.random.normal(ks[0], (BATCH, SEQ, D_MODEL), f32)
    v0 = 2 * D_ATTN_QK
    col_scale = jnp.ones((IN_WIDTH,), f32).at[v0:v0 + D_ATTN_V].set(BETA)
    w_in = nrm(ks[1], (L, D_MODEL, IN_WIDTH), D_MODEL) * col_scale
    b_gate = 0.01 * jax.random.normal(ks[2], (L, GATE_W), f32)
    lambda_q1 = 0.1 * jax.random.normal(ks[3], (L, ATTN_HEAD_DIM), f32)
    lambda_k1 = 0.1 * jax.random.normal(ks[4], (L, ATTN_HEAD_DIM), f32)
    lambda_q2 = 0.1 * jax.random.normal(ks[5], (L, ATTN_HEAD_DIM), f32)
    lambda_k2 = 0.1 * jax.random.normal(ks[6], (L, ATTN_HEAD_DIM), f32)
    attn_subln_w = 1.0 + 0.02 * jax.random.normal(ks[7], (L, ATTN_V_DIM), f32)
    conv_w = nrm(ks[8], (L, CONV_WIDTH, D_CONV_CH), CONV_WIDTH)
    conv_b = 0.01 * jax.random.normal(ks[9], (L, D_CONV_CH), f32)

    def dt_bias(k):
        u = jax.random.uniform(k, (L, SSM_HEADS), f32)
        dt = jnp.exp(u * (math.log(0.1) - math.log(0.001)) + math.log(0.001))
        return dt + jnp.log(-jnp.expm1(-dt))

    dt_bias_fwd = dt_bias(ks[10])
    dt_bias_bwd = dt_bias(ks[11])
    a_log_fwd = jnp.log(jax.random.uniform(ks[12], (L, SSM_HEADS), f32, 1.0, 16.0))
    a_log_bwd = jnp.log(jax.random.uniform(ks[13], (L, SSM_HEADS), f32, 1.0, 16.0))
    d_skip = 1.0 + 0.1 * jax.random.normal(ks[14], (L, SSM_HEADS), f32)
    ssm_norm_w = 1.0 + 0.02 * jax.random.normal(ks[15], (L, D_SSM), f32)
    w_branch = nrm(ks[16], (L, D_ATTN_V + D_SSM, D_MODEL), D_ATTN_V, BETA)
    w_out = nrm(ks[17], (L, D_MODEL, D_MODEL), D_MODEL, BETA)
    ln1_g = 1.0 + 0.02 * jax.random.normal(ks[18], (L, D_MODEL), f32)
    ln1_b = 0.01 * jax.random.normal(ks[19], (L, D_MODEL), f32)
    w_router = nrm(ks[20], (L, D_MODEL, N_EXPERTS), D_MODEL)
    w_gate_e = nrm(ks[21], (L, N_EXPERTS, D_MODEL, EXPERT_FF), D_MODEL)
    w_up_e = nrm(ks[22], (L, N_EXPERTS, D_MODEL, EXPERT_FF), D_MODEL)
    w_down_e = nrm(ks[23], (L, N_EXPERTS, EXPERT_FF, D_MODEL), EXPERT_FF, BETA)
    ln2_g = 1.0 + 0.02 * jax.random.normal(ks[24], (L, D_MODEL), f32)
    ln2_b = 0.01 * jax.random.normal(ks[25], (L, D_MODEL), f32)
    return {'x': x, 'w_in': w_in, 'b_gate': b_gate, 'lambda_q1': lambda_q1, 'lambda_k1': lambda_k1,
            'lambda_q2': lambda_q2, 'lambda_k2': lambda_k2, 'attn_subln_w': attn_subln_w,
            'conv_w': conv_w, 'conv_b': conv_b, 'dt_bias_fwd': dt_bias_fwd, 'dt_bias_bwd': dt_bias_bwd,
            'a_log_fwd': a_log_fwd, 'a_log_bwd': a_log_bwd, 'd_skip': d_skip, 'ssm_norm_w': ssm_norm_w,
            'w_branch': w_branch, 'w_out': w_out, 'ln1_g': ln1_g, 'ln1_b': ln1_b, 'w_router': w_router,
            'w_gate_e': w_gate_e, 'w_up_e': w_up_e, 'w_down_e': w_down_e, 'ln2_g': ln2_g, 'ln2_b': ln2_b}


def reference(x, w_in, b_gate, lambda_q1, lambda_k1, lambda_q2, lambda_k2, attn_subln_w,
              conv_w, conv_b, dt_bias_fwd, dt_bias_bwd, a_log_fwd, a_log_bwd, d_skip, ssm_norm_w,
              w_branch, w_out, ln1_g, ln1_b, w_router, w_gate_e, w_up_e, w_down_e, ln2_g, ln2_b):
    for l in range(DEPTH):
        x = hybrid_layer(x, w_in[l], b_gate[l], lambda_q1[l], lambda_k1[l], lambda_q2[l], lambda_k2[l],
                         attn_subln_w[l], conv_w[l], conv_b[l], dt_bias_fwd[l], dt_bias_bwd[l],
                         a_log_fwd[l], a_log_bwd[l], d_skip[l], ssm_norm_w[l], w_branch[l], w_out[l],
                         ln1_g[l], ln1_b[l], w_router[l], w_gate_e[l], w_up_e[l], w_down_e[l],
                         ln2_g[l], ln2_b[l], l)
    return x
```

```python
import functools
import math

import jax
import jax.numpy as jnp
from jax import lax
from jax.experimental import pallas as pl
from jax.experimental.pallas import tpu as pltpu

F32 = jnp.float32
BF16 = jnp.bfloat16

ATTN_HEADS = 16
ATTN_HEAD_DIM = 64
ATTN_V_DIM = 128
ROPE_THETA = 10000.0
SSM_HEAD_DIM = 64
SSM_HEADS = 32
SSM_GROUPS = 8
SSM_STATE = 128
SSM_CHUNK = 128
CONV_WIDTH = 5
N_EXPERTS = 16
CAPACITY_FACTOR = 2
DEPTH = 1
ALPHA = (2.0 * DEPTH) ** 0.25

LANES = 128
SUBLANES = 8
VMEM_LIMIT = 56 * 1024 * 1024

HEADS_PER_GROUP = SSM_HEADS // SSM_GROUPS
GROUP_W = HEADS_PER_GROUP * SSM_HEAD_DIM
CONV_PAD_ROWS = SUBLANES


def _cparams(sem):
    return pltpu.CompilerParams(dimension_semantics=sem, vmem_limit_bytes=VMEM_LIMIT)


def _sigmoid(x):
    return 1.0 / (1.0 + jnp.exp(-x))


def _silu(x):
    return x * _sigmoid(x)


def _softplus(x):
    return jnp.maximum(x, 0.0) + jnp.log1p(jnp.exp(-jnp.abs(x)))


def _layer_norm(r, g, b):
    mu = jnp.mean(r, axis=-1, keepdims=True)
    d = r - mu
    var = jnp.mean(d * d, axis=-1, keepdims=True)
    return d * lax.rsqrt(var + 1e-5) * g + b


def _mm_kernel(a_ref, w_ref, o_ref):
    o_ref[...] = jnp.dot(a_ref[...], w_ref[...], preferred_element_type=F32).astype(o_ref.dtype)


def _matmul(a, w, out_dtype, tm, tn):
    m, k = a.shape
    n = w.shape[1]
    return pl.pallas_call(
        _mm_kernel,
        grid=(n // tn, m // tm),
        in_specs=[pl.BlockSpec((tm, k), lambda j, i: (i, 0)),
                  pl.BlockSpec((k, tn), lambda j, i: (0, j))],
        out_specs=pl.BlockSpec((tm, tn), lambda j, i: (i, j)),
        out_shape=jax.ShapeDtypeStruct((m, n), out_dtype),
        compiler_params=_cparams(("arbitrary", "arbitrary")),
        name="in_proj",
    )(a, w)


def _dtprep_kernel(raw_ref, bias_ref, alog_ref, dt_ref, cs_ref):
    seq = raw_ref.shape[1]
    lc = SSM_CHUNK
    dt = _softplus(raw_ref[0] + bias_ref[...])
    dt_ref[0] = dt
    la = dt * (-jnp.exp(alog_ref[...]))
    row = lax.broadcasted_iota(jnp.int32, (lc, lc), 0)
    col = lax.broadcasted_iota(jnp.int32, (lc, lc), 1)
    t_low = jnp.where(row >= col, 1.0, 0.0).astype(F32)
    t_up = jnp.where(row <= col, 1.0, 0.0).astype(F32)
    fwd_lane = lax.broadcasted_iota(jnp.int32, (1, LANES), 1) < SSM_HEADS
    for c in range(seq // lc):
        lac = la[c * lc:(c + 1) * lc]
        f = jnp.dot(t_low, lac, preferred_element_type=F32, precision=lax.Precision.HIGHEST)
        b = jnp.dot(t_up, lac, preferred_element_type=F32, precision=lax.Precision.HIGHEST)
        cs_ref[0, c * lc:(c + 1) * lc, :] = jnp.where(fwd_lane, f, b)


def _dtprep(raw, bias, alog):
    bsz, seq, _ = raw.shape
    blk = pl.BlockSpec((1, seq, LANES), lambda b: (b, 0, 0))
    par = pl.BlockSpec((1, LANES), lambda b: (0, 0))
    return pl.pallas_call(
        _dtprep_kernel,
        grid=(bsz,),
        in_specs=[blk, par, par],
        out_specs=[blk, blk],
        out_shape=[jax.ShapeDtypeStruct(raw.shape, F32)] * 2,
        compiler_params=_cparams(("arbitrary",)),
        name="dt_prep",
    )(raw, bias, alog)


def _attn_kernel(lq1_ref, lk1_ref, lq2_ref, lk2_ref, sw_ref, cq_ref, sq_ref, ck_ref, sk_ref,
                 q_ref, k_ref, v_ref, o_ref, kr_ref, vb_ref, *, lambda_init):
    tq = q_ref.shape[1]
    lane = lax.broadcasted_iota(jnp.int32, (1, LANES), 1)
    first_half = (lane & (ATTN_HEAD_DIM // 2)) == 0
    comp1 = lane < ATTN_HEAD_DIM

    def rope(x, c, s):
        partner = jnp.where(first_half,
                            pltpu.roll(x, LANES - ATTN_HEAD_DIM // 2, 1),
                            pltpu.roll(x, ATTN_HEAD_DIM // 2, 1))
        return x * c + partner * s

    @pl.when(pl.program_id(2) == 0)
    def _():
        kr_ref[...] = rope(k_ref[0], ck_ref[...], sk_ref[...]).astype(BF16)
        vb_ref[...] = v_ref[0].astype(BF16)

    lam = (jnp.exp(jnp.sum(lq1_ref[...] * lk1_ref[...], axis=-1, keepdims=True))
           - jnp.exp(jnp.sum(lq2_ref[...] * lk2_ref[...], axis=-1, keepdims=True)) + lambda_init)

    q = rope(q_ref[0], cq_ref[...], sq_ref[...]) * (ATTN_HEAD_DIM ** -0.5)
    qs = jnp.concatenate([jnp.where(comp1, q, 0.0), jnp.where(comp1, 0.0, q)], axis=0).astype(BF16)
    s = lax.dot_general(qs, kr_ref[...], (((1,), (1,)), ((), ())), preferred_element_type=F32)
    p = jnp.exp(s - jnp.max(s, axis=-1, keepdims=True))
    inv = 1.0 / jnp.sum(p, axis=-1, keepdims=True)
    pc = (p[:tq] * inv[:tq] - p[tq:] * (lam * inv[tq:])).astype(BF16)
    o = jnp.dot(pc, vb_ref[...], preferred_element_type=F32)
    o = o * lax.rsqrt(jnp.mean(o * o, axis=-1, keepdims=True) + 1e-6) * sw_ref[...]
    o_ref[0] = (o * (1.0 - lambda_init)).astype(o_ref.dtype)


def _rope_tables(seq):
    half = ATTN_HEAD_DIM // 2
    inv_freq = ROPE_THETA ** (-jnp.arange(0, ATTN_HEAD_DIM, 2, dtype=F32) / ATTN_HEAD_DIM)
    ang = jnp.arange(seq, dtype=F32)[:, None] * inv_freq[None, :]
    cos, sin = jnp.cos(ang), jnp.sin(ang)
    reps = LANES // half
    sign = jnp.tile(jnp.concatenate([-jnp.ones((half,), F32), jnp.ones((half,), F32)]), reps // 2)
    return jnp.tile(cos, (1, reps)), jnp.tile(sin, (1, reps)) * sign[None, :]


def _attention(proj, lq1, lk1, lq2, lk2, subln_w, lambda_init, tq):
    bsz, seq, _ = proj.shape
    cos, sin = _rope_tables(seq)
    vec = pl.BlockSpec((1, ATTN_HEAD_DIM), lambda b, h, i: (0, 0))
    return pl.pallas_call(
        functools.partial(_attn_kernel, lambda_init=lambda_init),
        grid=(bsz, ATTN_HEADS, seq // tq),
        in_specs=[vec, vec, vec, vec,
                  pl.BlockSpec((1, LANES), lambda b, h, i: (0, 0)),
                  pl.BlockSpec((tq, LANES), lambda b, h, i: (i, 0)),
                  pl.BlockSpec((tq, LANES), lambda b, h, i: (i, 0)),
                  pl.BlockSpec((seq, LANES), lambda b, h, i: (0, 0)),
                  pl.BlockSpec((seq, LANES), lambda b, h, i: (0, 0)),
                  pl.BlockSpec((1, tq, LANES), lambda b, h, i: (b, i, h)),
                  pl.BlockSpec((1, seq, LANES), lambda b, h, i: (b, 0, ATTN_HEADS + h)),
                  pl.BlockSpec((1, seq, LANES), lambda b, h, i: (b, 0, 2 * ATTN_HEADS + h))],
        out_specs=pl.BlockSpec((1, tq, LANES), lambda b, h, i: (b, i, h)),
        out_shape=jax.ShapeDtypeStruct((bsz, seq, ATTN_HEADS * ATTN_V_DIM), BF16),
        scratch_shapes=[pltpu.VMEM((seq, LANES), BF16), pltpu.VMEM((seq, LANES), BF16)],
        compiler_params=_cparams(("arbitrary", "arbitrary", "arbitrary")),
        name="diff_attention",
    )(lq1, lk1, lq2, lk2, subln_w, cos, sin, cos, sin, proj, proj, proj)


def _ssd_kernel(z_ref, x_ref, b_ref, c_ref, dt_ref, cs_ref, cst_ref,
                cwx_ref, cwb_ref, cwc_ref, cbx_ref, cbb_ref, cbc_ref, dsk_ref, nw_ref,
                o_ref, pad_ref, xs_ref, bm_ref, cm_ref, y_ref, h_ref):
    seq = x_ref.shape[1]
    lc = SSM_CHUNK
    n_chunks = seq // lc
    row_tile = 256

    def conv_silu(in_ref, w_ref, bias_ref, out_ref, width):
        zeros = jnp.zeros((CONV_PAD_ROWS, width), F32)
        pad_ref[0:CONV_PAD_ROWS, 0:width] = zeros
        pad_ref[CONV_PAD_ROWS + seq:2 * CONV_PAD_ROWS + seq, 0:width] = zeros
        pad_ref[CONV_PAD_ROWS:CONV_PAD_ROWS + seq, 0:width] = in_ref[0]
        half = (CONV_WIDTH - 1) // 2
        for t in range(seq // row_tile):
            acc = jnp.broadcast_to(bias_ref[...], (row_tile, width))
            for j in range(CONV_WIDTH):
                start = CONV_PAD_ROWS + t * row_tile + j - half
                acc = acc + pad_ref[start:start + row_tile, 0:width] * w_ref[j:j + 1, :]
            out_ref[t * row_tile:(t + 1) * row_tile, :] = _silu(acc)

    conv_silu(x_ref, cwx_ref, cbx_ref, xs_ref, GROUP_W)
    conv_silu(b_ref, cwb_ref, cbb_ref, bm_ref, SSM_STATE)
    conv_silu(c_ref, cwc_ref, cbc_ref, cm_ref, SSM_STATE)

    head_of_lane = lax.broadcasted_iota(jnp.int32, (1, GROUP_W), 1) // SSM_HEAD_DIM
    row = lax.broadcasted_iota(jnp.int32, (lc, lc), 0)
    col = lax.broadcasted_iota(jnp.int32, (lc, lc), 1)

    def expand(cols, off):
        out = cols[:, off + HEADS_PER_GROUP - 1:off + HEADS_PER_GROUP]
        for r in range(HEADS_PER_GROUP - 2, -1, -1):
            out = jnp.where(head_of_lane == r, cols[:, off + r:off + r + 1], out)
        return out

    def run_direction(reverse):
        off = HEADS_PER_GROUP if reverse else 0
        mask = (row <= col) if reverse else (row >= col)
        edge = 0 if reverse else lc - 1
        h_ref[...] = jnp.zeros_like(h_ref)

        def body(ci, carry):
            c = (n_chunks - 1 - ci) if reverse else ci
            r0 = pl.multiple_of(c * lc, lc)
            xc = xs_ref[pl.ds(r0, lc), :]
            bc = bm_ref[pl.ds(r0, lc), :]
            cc = cm_ref[pl.ds(r0, lc), :].astype(BF16)
            dtc = dt_ref[0, 0, pl.ds(r0, lc), :]
            csc = cs_ref[0, 0, pl.ds(r0, lc), :]
            cst = cst_ref[0, 0, c]
            cs_e = expand(csc, off)
            edge_e = expand(csc[edge:edge + 1, :], off)
            xdt = xc * expand(dtc, off)
            xdt_b = xdt.astype(BF16)
            cb = lax.dot_general(cc, bc.astype(BF16), (((1,), (1,)), ((), ())),
                                 preferred_element_type=F32)
            y = jnp.zeros((lc, GROUP_W), F32)
            for r in range(HEADS_PER_GROUP):
                diff = csc[:, off + r:off + r + 1] - cst[off + r:off + r + 1, :]
                decay = jnp.exp(jnp.where(mask, diff, -jnp.inf))
                yr = jnp.dot((cb * decay).astype(BF16), xdt_b, preferred_element_type=F32)
                y = jnp.where(head_of_lane == r, yr, y)
            h_t = h_ref[...]
            y = y + jnp.dot(cc, h_t.astype(BF16), preferred_element_type=F32) * jnp.exp(cs_e)
            new_state = jnp.dot(bc.T.astype(BF16), (xdt * jnp.exp(edge_e - cs_e)).astype(BF16),
                                preferred_element_type=F32)
            h_ref[...] = h_t * jnp.exp(edge_e) + new_state
            if reverse:
                y_ref[pl.ds(r0, lc), :] += y
            else:
                y_ref[pl.ds(r0, lc), :] = y
            return carry

        lax.fori_loop(0, n_chunks, body, 0)

    run_direction(False)
    run_direction(True)

    for t in range(seq // row_tile):
        rows = slice(t * row_tile, (t + 1) * row_tile)
        y = y_ref[rows, :] + xs_ref[rows, :] * dsk_ref[...]
        y = y * _silu(z_ref[0, rows, :])
        y = y * lax.rsqrt(jnp.mean(y * y, axis=-1, keepdims=True) + 1e-6) * nw_ref[...]
        o_ref[0, rows, :] = y.astype(o_ref.dtype)


def _ssd(proj, dtg, csg, cstg, conv_w, conv_b, dskip_e, norm_w, col0_z, col0_xbc):
    bsz, seq, _ = proj.shape
    d_ssm = SSM_HEADS * SSM_HEAD_DIM
    gn = SSM_GROUPS * SSM_STATE
    n_chunks = seq // SSM_CHUNK
    zb, xb = col0_z // GROUP_W, col0_xbc // GROUP_W
    bb, cb = (col0_xbc + d_ssm) // SSM_STATE, (col0_xbc + d_ssm + gn) // SSM_STATE
    wb, wc = d_ssm // SSM_STATE, (d_ssm + gn) // SSM_STATE
    n_dir_heads = 2 * HEADS_PER_GROUP
    return pl.pallas_call(
        _ssd_kernel,
        grid=(bsz, SSM_GROUPS),
        in_specs=[pl.BlockSpec((1, seq, GROUP_W), lambda b, g: (b, 0, zb + g)),
                  pl.BlockSpec((1, seq, GROUP_W), lambda b, g: (b, 0, xb + g)),
                  pl.BlockSpec((1, seq, SSM_STATE), lambda b, g: (b, 0, bb + g)),
                  pl.BlockSpec((1, seq, SSM_STATE), lambda b, g: (b, 0, cb + g)),
                  pl.BlockSpec((1, 1, seq, n_dir_heads), lambda b, g: (b, g, 0, 0)),
                  pl.BlockSpec((1, 1, seq, n_dir_heads), lambda b, g: (b, g, 0, 0)),
                  pl.BlockSpec((1, 1, n_chunks, n_dir_heads, SSM_CHUNK), lambda b, g: (b, g, 0, 0, 0)),
                  pl.BlockSpec((CONV_WIDTH, GROUP_W), lambda b, g: (0, g)),
                  pl.BlockSpec((CONV_WIDTH, SSM_STATE), lambda b, g: (0, wb + g)),
                  pl.BlockSpec((CONV_WIDTH, SSM_STATE), lambda b, g: (0, wc + g)),
                  pl.BlockSpec((1, GROUP_W), lambda b, g: (0, g)),
                  pl.BlockSpec((1, SSM_STATE), lambda b, g: (0, wb + g)),
                  pl.BlockSpec((1, SSM_STATE), lambda b, g: (0, wc + g)),
                  pl.BlockSpec((1, GROUP_W), lambda b, g: (0, g)),
                  pl.BlockSpec((1, GROUP_W), lambda b, g: (0, g))],
        out_specs=pl.BlockSpec((1, seq, GROUP_W), lambda b, g: (b, 0, g)),
        out_shape=jax.ShapeDtypeStruct((bsz, seq, d_ssm), BF16),
        scratch_shapes=[pltpu.VMEM((seq + 2 * CONV_PAD_ROWS, GROUP_W), F32),
                        pltpu.VMEM((seq, GROUP_W), F32),
                        pltpu.VMEM((seq, SSM_STATE), F32),
                        pltpu.VMEM((seq, SSM_STATE), F32),
                        pltpu.VMEM((seq, GROUP_W), F32),
                        pltpu.VMEM((SSM_STATE, GROUP_W), F32)],
        compiler_params=_cparams(("arbitrary", "arbitrary")),
        name="ssd",
    )(proj, proj, proj, proj, dtg, csg, cstg, conv_w, conv_w, conv_w, conv_b, conv_b, conv_b,
      dskip_e, norm_w)


def _merge_kernel(x_ref, wga_ref, wgs_ref, bga_ref, bgs_ref, ya_ref, ys_ref, wba_ref, wbs_ref, u_ref):
    x = x_ref[...]
    ga = _sigmoid(jnp.dot(x, wga_ref[...], preferred_element_type=F32) + bga_ref[...])
    gs = _sigmoid(jnp.dot(x, wgs_ref[...], preferred_element_type=F32) + bgs_ref[...])
    ba = jnp.dot(ya_ref[...], wba_ref[...], preferred_element_type=F32)
    bs = jnp.dot(ys_ref[...], wbs_ref[...], preferred_element_type=F32)
    u_ref[...] = (ga * ba + gs * bs).astype(u_ref.dtype)


def _merge(xb, w_gate, b_gate, y_attn, y_ssm, w_branch, tm, tn):
    m, d = xb.shape
    n = w_branch.shape[1]
    ka, ks = y_attn.shape[1], y_ssm.shape[1]
    nj = n // tn
    return pl.pallas_call(
        _merge_kernel,
        grid=(nj, m // tm),
        in_specs=[pl.BlockSpec((tm, d), lambda j, i: (i, 0)),
                  pl.BlockSpec((d, tn), lambda j, i: (0, j)),
                  pl.BlockSpec((d, tn), lambda j, i: (0, nj + j)),
                  pl.BlockSpec((1, tn), lambda j, i: (0, j)),
                  pl.BlockSpec((1, tn), lambda j, i: (0, nj + j)),
                  pl.BlockSpec((tm, ka), lambda j, i: (i, 0)),
                  pl.BlockSpec((tm, ks), lambda j, i: (i, 0)),
                  pl.BlockSpec((ka, tn), lambda j, i: (0, j)),
                  pl.BlockSpec((ks, tn), lambda j, i: (ka // ks, j))],
        out_specs=pl.BlockSpec((tm, tn), lambda j, i: (i, j)),
        out_shape=jax.ShapeDtypeStruct((m, n), BF16),
        compiler_params=_cparams(("arbitrary", "arbitrary")),
        name="gated_merge",
    )(xb, w_gate, w_gate, b_gate, b_gate, y_attn, y_ssm, w_branch, w_branch)


def _outproj_kernel(u_ref, w_ref, x_ref, g_ref, b_ref, wr_ref, x1_ref, x1b_ref, lg_ref):
    k = pl.program_id(1)
    part = jnp.dot(u_ref[...], w_ref[...], preferred_element_type=F32)

    @pl.when(k == 0)
    def _():
        x1_ref[...] = part

    @pl.when(k > 0)
    def _():
        x1_ref[...] += part

    @pl.when(k == pl.num_programs(1) - 1)
    def _():
        x1 = _layer_norm(ALPHA * x_ref[...] + x1_ref[...], g_ref[...], b_ref[...])
        x1_ref[...] = x1
        x1b_ref[...] = x1.astype(BF16)
        lg_ref[...] = jnp.dot(x1, wr_ref[...], preferred_element_type=F32,
                              precision=lax.Precision.HIGHEST)


def _outproj_ln(u, w_out, x, g, b, w_router_p, tm, tk):
    m, kdim = u.shape
    d = w_out.shape[1]
    row = pl.BlockSpec((tm, d), lambda i, k: (i, 0))
    par = pl.BlockSpec((1, d), lambda i, k: (0, 0))
    return pl.pallas_call(
        _outproj_kernel,
        grid=(m // tm, kdim // tk),
        in_specs=[pl.BlockSpec((tm, tk), lambda i, k: (i, k)),
                  pl.BlockSpec((tk, d), lambda i, k: (k, 0)),
                  row, par, par,
                  pl.BlockSpec((d, LANES), lambda i, k: (0, 0))],
        out_specs=[row, row, pl.BlockSpec((tm, LANES), lambda i, k: (i, 0))],
        out_shape=[jax.ShapeDtypeStruct((m, d), F32), jax.ShapeDtypeStruct((m, d), BF16),
                   jax.ShapeDtypeStruct((m, LANES), F32)],
        compiler_params=_cparams(("arbitrary", "arbitrary")),
        name="out_proj_ln1",
    )(u, w_out, x, g, b, w_router_p)


def _routing_kernel(lg_ref, slot_ref, slott_ref, gslot_ref, aff_ref, *, cap):
    seq = lg_ref.shape[1]
    blk = 256
    lg = lg_ref[0]
    valid = lax.broadcasted_iota(jnp.int32, lg.shape, 1) < N_EXPERTS
    lgm = jnp.where(valid, lg, -jnp.inf)
    ex = jnp.exp(lgm - jnp.max(lgm, axis=-1, keepdims=True))
    aff = ex / jnp.sum(ex, axis=-1, keepdims=True)
    aff_t = aff.T[0:N_EXPERTS]
    aff_ref[...] = aff_t
    bits = lax.bitcast_convert_type(aff_t, jnp.int32)

    def count(m):
        return jnp.sum(jnp.where(m, 1.0, 0.0), axis=-1, keepdims=True)

    def search(i, thr):
        cand = thr | jnp.left_shift(jnp.int32(1), 30 - i)
        return jnp.where(count(bits >= cand) >= cap, cand, thr)

    thr = lax.fori_loop(0, 31, search, jnp.zeros((N_EXPERTS, 1), jnp.int32))
    gt = bits > thr
    eq = bits == thr

    r_i = lax.broadcasted_iota(jnp.int32, (blk, blk), 0)
    c_i = lax.broadcasted_iota(jnp.int32, (blk, blk), 1)
    before = jnp.where(r_i < c_i, 1.0, 0.0).astype(BF16)

    def excl_cumsum(m):
        mf = jnp.where(m, 1.0, 0.0)
        carry = jnp.zeros((N_EXPERTS, 1), F32)
        parts = []
        for k in range(seq // blk):
            piece = mf[:, k * blk:(k + 1) * blk]
            parts.append(jnp.dot(piece.astype(BF16), before, preferred_element_type=F32) + carry)
            carry = carry + jnp.sum(piece, axis=-1, keepdims=True)
        return jnp.concatenate(parts, axis=-1)

    need = cap - count(gt)
    sel = gt | (eq & (excl_cumsum(eq) < need))
    slot = jnp.where(sel, excl_cumsum(sel), -1.0)
    slot_ref[0] = slot
    pad = jnp.full((LANES - N_EXPERTS, seq), -1.0, F32)
    slott_ref[0] = jnp.concatenate([slot, pad], axis=0).T

    j_iota = lax.broadcasted_iota(jnp.int32, (cap, seq), 0).astype(F32)

    def gate_of_slot(e, carry):
        hit = slot_ref[0, pl.ds(e, 1), :] == j_iota
        gslot_ref[0, e] = jnp.sum(jnp.where(hit, aff_ref[pl.ds(e, 1), :], 0.0), axis=-1, keepdims=True)
        return carry

    lax.fori_loop(0, N_EXPERTS, gate_of_slot, 0)


def _routing(logits, cap):
    bsz, seq, _ = logits.shape
    return pl.pallas_call(
        functools.partial(_routing_kernel, cap=cap),
        grid=(bsz,),
        in_specs=[pl.BlockSpec((1, seq, LANES), lambda b: (b, 0, 0))],
        out_specs=[pl.BlockSpec((1, N_EXPERTS, seq), lambda b: (b, 0, 0)),
                   pl.BlockSpec((1, seq, LANES), lambda b: (b, 0, 0)),
                   pl.BlockSpec((1, N_EXPERTS, cap, 1), lambda b: (b, 0, 0, 0))],
        out_shape=[jax.ShapeDtypeStruct((bsz, N_EXPERTS, seq), F32),
                   jax.ShapeDtypeStruct((bsz, seq, LANES), F32),
                   jax.ShapeDtypeStruct((bsz, N_EXPERTS, cap, 1), F32)],
        scratch_shapes=[pltpu.VMEM((N_EXPERTS, seq), F32)],
        compiler_params=_cparams(("arbitrary",)),
        name="routing",
    )(logits)


def _gather_kernel(slot_ref, x_ref, o_ref, *, cap):
    e = pl.program_id(2)
    seq = x_ref.shape[1]
    j_iota = lax.broadcasted_iota(jnp.int32, (cap, seq), 0).astype(F32)
    pick = jnp.where(slot_ref[0, pl.ds(e, 1), :] == j_iota, 1.0, 0.0).astype(BF16)
    o_ref[0] = jnp.dot(pick, x_ref[0], preferred_element_type=F32).astype(o_ref.dtype)


def _gather(slot, x1b, cap, td):
    bsz, seq, d = x1b.shape
    return pl.pallas_call(
        functools.partial(_gather_kernel, cap=cap),
        grid=(bsz, d // td, N_EXPERTS),
        in_specs=[pl.BlockSpec((1, N_EXPERTS, seq), lambda b, j, e: (b, 0, 0)),
                  pl.BlockSpec((1, seq, td), lambda b, j, e: (b, 0, j))],
        out_specs=pl.BlockSpec((1, cap, td), lambda b, j, e: (e, b, j)),
        out_shape=jax.ShapeDtypeStruct((N_EXPERTS, bsz * cap, d), BF16),
        compiler_params=_cparams(("arbitrary", "arbitrary", "arbitrary")),
        name="moe_gather",
    )(slot, x1b)


def _gateup_kernel(xg_ref, wg_ref, wu_ref, h_ref):
    xg = xg_ref[0]
    g = jnp.dot(xg, wg_ref[0].astype(BF16), preferred_element_type=F32)
    u = jnp.dot(xg, wu_ref[0].astype(BF16), preferred_element_type=F32)
    h_ref[0] = (_silu(g) * u).astype(h_ref.dtype)


def _gateup(xg, w_gate_e, w_up_e, tf):
    n_e, rows, d = xg.shape
    ff = w_gate_e.shape[2]
    wspec = pl.BlockSpec((1, d, tf), lambda e, f: (e, 0, f))
    return pl.pallas_call(
        _gateup_kernel,
        grid=(n_e, ff // tf),
        in_specs=[pl.BlockSpec((1, rows, d), lambda e, f: (e, 0, 0)), wspec, wspec],
        out_specs=pl.BlockSpec((1, rows, tf), lambda e, f: (e, 0, f)),
        out_shape=jax.ShapeDtypeStruct((n_e, rows, ff), BF16),
        compiler_params=_cparams(("arbitrary", "arbitrary")),
        name="moe_gate_up",
    )(xg, w_gate_e, w_up_e)


def _down_kernel(h_ref, wd_ref, gs_ref, y_ref):
    y = jnp.dot(h_ref[0], wd_ref[0].astype(BF16), preferred_element_type=F32)
    y_ref[0] = (y * gs_ref[0]).astype(y_ref.dtype)


def _down(h, w_down_e, gslot, td):
    n_e, rows, ff = h.shape
    d = w_down_e.shape[2]
    return pl.pallas_call(
        _down_kernel,
        grid=(n_e, d // td),
        in_specs=[pl.BlockSpec((1, rows, ff), lambda e, j: (e, 0, 0)),
                  pl.BlockSpec((1, ff, td), lambda e, j: (e, 0, j)),
                  pl.BlockSpec((1, rows, 1), lambda e, j: (e, 0, 0))],
        out_specs=pl.BlockSpec((1, rows, td), lambda e, j: (e, 0, j)),
        out_shape=jax.ShapeDtypeStruct((n_e, rows, d), BF16),
        compiler_params=_cparams(("arbitrary", "arbitrary")),
        name="moe_down",
    )(h, w_down_e, gslot)


def _scatter_kernel(slott_ref, yg_ref, x1_ref, g_ref, b_ref, o_ref, *, cap, ek):
    k = pl.program_id(2)
    ts = o_ref.shape[1]
    st = slott_ref[0]
    lane = lax.broadcasted_iota(jnp.int32, st.shape, 1)
    j_iota = lax.broadcasted_iota(jnp.int32, (ts, cap), 1).astype(F32)
    acc = jnp.zeros(o_ref.shape[1:], F32)
    for i in range(ek):
        col = jnp.sum(jnp.where(lane == k * ek + i, st, 0.0), axis=-1, keepdims=True)
        put = jnp.where(col == j_iota, 1.0, 0.0).astype(BF16)
        acc = acc + jnp.dot(put, yg_ref[i], preferred_element_type=F32)

    @pl.when(k == 0)
    def _():
        o_ref[0] = acc

    @pl.when(k > 0)
    def _():
        o_ref[0] += acc

    @pl.when(k == pl.num_programs(2) - 1)
    def _():
        o_ref[0] = _layer_norm(ALPHA * x1_ref[0] + o_ref[0], g_ref[...], b_ref[...])


def _scatter_ln(slot_t, yg, x1, g, b, cap, ts, ek):
    bsz, seq, d = x1.shape
    par = pl.BlockSpec((1, d), lambda bi, i, k: (0, 0))
    return pl.pallas_call(
        functools.partial(_scatter_kernel, cap=cap, ek=ek),
        grid=(bsz, seq // ts, N_EXPERTS // ek),
        in_specs=[pl.BlockSpec((1, ts, LANES), lambda bi, i, k: (bi, i, 0)),
                  pl.BlockSpec((ek, cap, d), lambda bi, i, k: (k, bi, 0)),
                  pl.BlockSpec((1, ts, d), lambda bi, i, k: (bi, i, 0)),
                  par, par],
        out_specs=pl.BlockSpec((1, ts, d), lambda bi, i, k: (bi, i, 0)),
        out_shape=jax.ShapeDtypeStruct((bsz, seq, d), F32),
        compiler_params=_cparams(("arbitrary", "arbitrary", "arbitrary")),
        name="moe_scatter_ln2",
    )(slot_t, yg, x1, g, b)


def _group_heads(t):
    bsz, seq, _ = t.shape
    t = t[:, :, :2 * SSM_HEADS].reshape(bsz, seq, 2, SSM_GROUPS, HEADS_PER_GROUP)
    return jnp.transpose(t, (0, 3, 1, 2, 4)).reshape(bsz, SSM_GROUPS, seq, 2 * HEADS_PER_GROUP)


def _layer(x, w_in, b_gate, lq1, lk1, lq2, lk2, subln_w, conv_w, conv_b, dtb_f, dtb_b, alog_f, alog_b,
           d_skip, ssm_norm_w, w_branch, w_out, ln1_g, ln1_b, w_router, w_gate_e, w_up_e, w_down_e,
           ln2_g, ln2_b, layer_idx):
    bsz, seq, d = x.shape
    m = bsz * seq
    d_qk = ATTN_HEADS * 2 * ATTN_HEAD_DIM
    d_v = ATTN_HEADS * ATTN_V_DIM
    d_ssm = SSM_HEADS * SSM_HEAD_DIM
    d_conv = d_ssm + 2 * SSM_GROUPS * SSM_STATE
    n_main = 2 * d_qk + d_v + d_ssm + d_conv
    n_dt = 2 * SSM_HEADS
    lambda_init = 0.8 - 0.6 * math.exp(-0.3 * layer_idx)
    cap = CAPACITY_FACTOR * seq // N_EXPERTS
    row = lambda v: v.reshape(1, -1)

    xb = x.reshape(m, d).astype(BF16)
    w_main = w_in[:, :n_main].astype(BF16)
    w_dt = jnp.pad(w_in[:, n_main:n_main + n_dt], ((0, 0), (0, LANES - n_dt))).astype(BF16)
    w_gate = w_in[:, n_main + n_dt:].astype(BF16)

    proj = _matmul(xb, w_main, F32, 512, 1024).reshape(bsz, seq, n_main)
    dt_raw = _matmul(xb, w_dt, F32, 512, LANES).reshape(bsz, seq, LANES)

    lane_pad = lambda a, bvec: jnp.pad(jnp.concatenate([a, bvec]), (0, LANES - n_dt)).reshape(1, LANES)
    dt, cs = _dtprep(dt_raw, lane_pad(dtb_f, dtb_b), lane_pad(alog_f, alog_b))
    dtg, csg = _group_heads(dt), _group_heads(cs)
    n_chunks = seq // SSM_CHUNK
    cstg = jnp.transpose(csg.reshape(bsz, SSM_GROUPS, n_chunks, SSM_CHUNK, 2 * HEADS_PER_GROUP),
                         (0, 1, 2, 4, 3))

    y_attn = _attention(proj, row(lq1), row(lk1), row(lq2), row(lk2), row(subln_w), lambda_init, 256)
    y_ssm = _ssd(proj, dtg, csg, cstg, conv_w, row(conv_b), row(jnp.repeat(d_skip, SSM_HEAD_DIM)),
                 row(ssm_norm_w), 2 * d_qk + d_v, 2 * d_qk + d_v + d_ssm)

    u = _merge(xb, w_gate, row(b_gate), y_attn.reshape(m, d_v), y_ssm.reshape(m, d_ssm),
               w_branch.astype(BF16), 512, 512)
    w_router_p = jnp.pad(w_router, ((0, 0), (0, LANES - N_EXPERTS)))
    x1, x1b, logits = _outproj_ln(u, w_out.astype(BF16), x.reshape(m, d), row(ln1_g), row(ln1_b),
                                  w_router_p, 256, 512)

    slot, slot_t, gslot = _routing(logits.reshape(bsz, seq, LANES), cap)
    xg = _gather(slot, x1b.reshape(bsz, seq, d), cap, 1024)
    h = _gateup(xg, w_gate_e, w_up_e, 256)
    gslot_e = jnp.transpose(gslot, (1, 0, 2, 3)).reshape(N_EXPERTS, bsz * cap, 1)
    yg = _down(h, w_down_e, gslot_e, 512)
    return _scatter_ln(slot_t, yg, x1.reshape(bsz, seq, d), row(ln2_g), row(ln2_b), cap, 256, 4)


def kernel(x, w_in, b_gate, lambda_q1, lambda_k1, lambda_q2, lambda_k2, attn_subln_w, conv_w, conv_b,
           dt_bias_fwd, dt_bias_bwd, a_log_fwd, a_log_bwd, d_skip, ssm_norm_w, w_branch, w_out,
           ln1_g, ln1_b, w_router, w_gate_e, w_up_e, w_down_e, ln2_g, ln2_b):
    for l in range(w_in.shape[0]):
        x = _layer(x, w_in[l], b_gate[l], lambda_q1[l], lambda_k1[l], lambda_q2[l], lambda_k2[l],
                   attn_subln_w[l], conv_w[l], conv_b[l], dt_bias_fwd[l], dt_bias_bwd[l],
                   a_log_fwd[l], a_log_bwd[l], d_skip[l], ssm_norm_w[l], w_branch[l], w_out[l],
                   ln1_g[l], ln1_b[l], w_router[l], w_gate_e[l], w_up_e[l], w_down_e[l],
                   ln2_g[l], ln2_b[l], l)
    return x
```

```python
import functools
import math

import jax
import jax.numpy as jnp
from jax import lax
from jax.experimental import pallas as pl
from jax.experimental.pallas import tpu as pltpu

F32 = jnp.float32
BF16 = jnp.bfloat16

ATTN_HEADS = 16
ATTN_HEAD_DIM = 64
ATTN_V_DIM = 128
ROPE_THETA = 10000.0
LOG2_E = math.log2(math.e)
ATTN_ROW_SPLITS = 2
SSM_HEAD_DIM = 64
SSM_HEADS = 32
SSM_GROUPS = 8
SSM_STATE = 128
SSM_CHUNK = 128
CONV_WIDTH = 5
N_EXPERTS = 16
CAPACITY_FACTOR = 2
DEPTH = 1
ALPHA = (2.0 * DEPTH) ** 0.25

LANES = 128
SUBLANES = 8
VMEM_LIMIT = 56 * 1024 * 1024

HEADS_PER_GROUP = SSM_HEADS // SSM_GROUPS
GROUP_W = HEADS_PER_GROUP * SSM_HEAD_DIM
CONV_PAD_ROWS = SUBLANES


def _cparams(sem):
    return pltpu.CompilerParams(dimension_semantics=sem, vmem_limit_bytes=VMEM_LIMIT)


def _sigmoid(x):
    return 1.0 / (1.0 + jnp.exp(-x))


def _silu(x):
    return x * _sigmoid(x)


def _softplus(x):
    return jnp.maximum(x, 0.0) + jnp.log1p(jnp.exp(-jnp.abs(x)))


def _layer_norm(r, g, b):
    mu = jnp.mean(r, axis=-1, keepdims=True)
    d = r - mu
    var = jnp.mean(d * d, axis=-1, keepdims=True)
    return d * lax.rsqrt(var + 1e-5) * g + b


def _mm_kernel(a_ref, w_ref, o_ref):
    o_ref[...] = jnp.dot(a_ref[...], w_ref[...], preferred_element_type=F32).astype(o_ref.dtype)


def _matmul(a, w, out_dtype, tm, tn):
    m, k = a.shape
    n = w.shape[1]
    return pl.pallas_call(
        _mm_kernel,
        grid=(n // tn, m // tm),
        in_specs=[pl.BlockSpec((tm, k), lambda j, i: (i, 0)),
                  pl.BlockSpec((k, tn), lambda j, i: (0, j))],
        out_specs=pl.BlockSpec((tm, tn), lambda j, i: (i, j)),
        out_shape=jax.ShapeDtypeStruct((m, n), out_dtype),
        compiler_params=_cparams(("arbitrary", "arbitrary")),
        name="dt_proj",
    )(a, w)


def _mm_wcast_kernel(a_ref, w_ref, o_ref, wb_ref):
    @pl.when(pl.program_id(1) == 0)
    def _():
        wb_ref[...] = w_ref[...].astype(BF16)

    o_ref[...] = jnp.dot(a_ref[...], wb_ref[...], preferred_element_type=F32).astype(o_ref.dtype)


def _matmul_f32w(a, w, n, out_dtype, tm, tn):
    m, k = a.shape
    return pl.pallas_call(
        _mm_wcast_kernel,
        grid=(n // tn, m // tm),
        in_specs=[pl.BlockSpec((tm, k), lambda j, i: (i, 0)),
                  pl.BlockSpec((k, tn), lambda j, i: (0, j))],
        out_specs=pl.BlockSpec((tm, tn), lambda j, i: (i, j)),
        out_shape=jax.ShapeDtypeStruct((m, n), out_dtype),
        scratch_shapes=[pltpu.VMEM((k, tn), BF16)],
        compiler_params=_cparams(("arbitrary", "arbitrary")),
        name="in_proj",
    )(a, w)


def _dtprep_kernel(raw_ref, bias_ref, alog_ref, dt_ref, cs_ref):
    seq = raw_ref.shape[1]
    lc = SSM_CHUNK
    dt = _softplus(raw_ref[0] + bias_ref[...])
    dt_ref[0] = dt
    la = dt * (-jnp.exp(alog_ref[...]))
    row = lax.broadcasted_iota(jnp.int32, (lc, lc), 0)
    col = lax.broadcasted_iota(jnp.int32, (lc, lc), 1)
    t_low = jnp.where(row >= col, 1.0, 0.0).astype(F32)
    t_up = jnp.where(row <= col, 1.0, 0.0).astype(F32)
    fwd_lane = lax.broadcasted_iota(jnp.int32, (1, LANES), 1) < SSM_HEADS
    for c in range(seq // lc):
        lac = la[c * lc:(c + 1) * lc]
        f = jnp.dot(t_low, lac, preferred_element_type=F32, precision=lax.Precision.HIGHEST)
        b = jnp.dot(t_up, lac, preferred_element_type=F32, precision=lax.Precision.HIGHEST)
        cs_ref[0, c * lc:(c + 1) * lc, :] = jnp.where(fwd_lane, f, b)


def _dtprep(raw, bias, alog):
    bsz, seq, _ = raw.shape
    blk = pl.BlockSpec((1, seq, LANES), lambda b: (b, 0, 0))
    par = pl.BlockSpec((1, LANES), lambda b: (0, 0))
    return pl.pallas_call(
        _dtprep_kernel,
        grid=(bsz,),
        in_specs=[blk, par, par],
        out_specs=[blk, blk],
        out_shape=[jax.ShapeDtypeStruct(raw.shape, F32)] * 2,
        compiler_params=_cparams(("arbitrary",)),
        name="dt_prep",
    )(raw, bias, alog)


def _attn_kernel(lq1_ref, lk1_ref, lq2_ref, lk2_ref, sw_ref, cq_ref, sq_ref, ck_ref, sk_ref,
                 q_ref, k_ref, v_ref, o_ref, kr_ref, vb_ref, *, lambda_init):
    tq = q_ref.shape[1]
    lane = lax.broadcasted_iota(jnp.int32, (1, LANES), 1)
    first_half = (lane & (ATTN_HEAD_DIM // 2)) == 0
    comp1 = lane < ATTN_HEAD_DIM

    def rope(x, c, s):
        partner = jnp.where(first_half,
                            pltpu.roll(x, LANES - ATTN_HEAD_DIM // 2, 1),
                            pltpu.roll(x, ATTN_HEAD_DIM // 2, 1))
        return x * c + partner * s

    @pl.when(pl.program_id(2) == 0)
    def _():
        kr_ref[...] = rope(k_ref[0], ck_ref[...], sk_ref[...]).astype(BF16)
        vb_ref[...] = v_ref[0].astype(BF16)

    lam = (jnp.exp(jnp.sum(lq1_ref[...] * lk1_ref[...], axis=-1, keepdims=True))
           - jnp.exp(jnp.sum(lq2_ref[...] * lk2_ref[...], axis=-1, keepdims=True)) + lambda_init)

    q = rope(q_ref[0], cq_ref[...], sq_ref[...]) * (ATTN_HEAD_DIM ** -0.5 * LOG2_E)
    q1 = jnp.where(comp1, q, 0.0).astype(BF16)
    q2 = jnp.where(comp1, 0.0, q).astype(BF16)
    sub = tq // ATTN_ROW_SPLITS
    for t in range(ATTN_ROW_SPLITS):
        rows = slice(t * sub, (t + 1) * sub)
        qs = jnp.concatenate([q1[rows], q2[rows]], axis=0)
        s = lax.dot_general(qs, kr_ref[...], (((1,), (1,)), ((), ())), preferred_element_type=F32)
        p = jnp.exp2(s - jnp.max(s, axis=-1, keepdims=True))
        inv = 1.0 / jnp.sum(p, axis=-1, keepdims=True)
        pv = jnp.dot(p.astype(BF16), vb_ref[...], preferred_element_type=F32)
        o = pv[:sub] * inv[:sub] - pv[sub:] * (lam * inv[sub:])
        o = o * lax.rsqrt(jnp.mean(o * o, axis=-1, keepdims=True) + 1e-6) * sw_ref[...]
        o_ref[0, rows, :] = (o * (1.0 - lambda_init)).astype(o_ref.dtype)


def _rope_tables(seq):
    half = ATTN_HEAD_DIM // 2
    inv_freq = ROPE_THETA ** (-jnp.arange(0, ATTN_HEAD_DIM, 2, dtype=F32) / ATTN_HEAD_DIM)
    ang = jnp.arange(seq, dtype=F32)[:, None] * inv_freq[None, :]
    cos, sin = jnp.cos(ang), jnp.sin(ang)
    reps = LANES // half
    sign = jnp.tile(jnp.concatenate([-jnp.ones((half,), F32), jnp.ones((half,), F32)]), reps // 2)
    return jnp.tile(cos, (1, reps)), jnp.tile(sin, (1, reps)) * sign[None, :]


def _attention(proj, lq1, lk1, lq2, lk2, subln_w, lambda_init, tq):
    bsz, seq, _ = proj.shape
    cos, sin = _rope_tables(seq)
    vec = pl.BlockSpec((1, ATTN_HEAD_DIM), lambda b, h, i: (0, 0))
    return pl.pallas_call(
        functools.partial(_attn_kernel, lambda_init=lambda_init),
        grid=(bsz, ATTN_HEADS, seq // tq),
        in_specs=[vec, vec, vec, vec,
                  pl.BlockSpec((1, LANES), lambda b, h, i: (0, 0)),
                  pl.BlockSpec((tq, LANES), lambda b, h, i: (i, 0)),
                  pl.BlockSpec((tq, LANES), lambda b, h, i: (i, 0)),
                  pl.BlockSpec((seq, LANES), lambda b, h, i: (0, 0)),
                  pl.BlockSpec((seq, LANES), lambda b, h, i: (0, 0)),
                  pl.BlockSpec((1, tq, LANES), lambda b, h, i: (b, i, h)),
                  pl.BlockSpec((1, seq, LANES), lambda b, h, i: (b, 0, ATTN_HEADS + h)),
                  pl.BlockSpec((1, seq, LANES), lambda b, h, i: (b, 0, 2 * ATTN_HEADS + h))],
        out_specs=pl.BlockSpec((1, tq, LANES), lambda b, h, i: (b, i, h)),
        out_shape=jax.ShapeDtypeStruct((bsz, seq, ATTN_HEADS * ATTN_V_DIM), BF16),
        scratch_shapes=[pltpu.VMEM((seq, LANES), BF16), pltpu.VMEM((seq, ATTN_V_DIM), BF16)],
        compiler_params=_cparams(("arbitrary", "arbitrary", "arbitrary")),
        name="diff_attention",
    )(lq1, lk1, lq2, lk2, subln_w, cos, sin, cos, sin, proj, proj, proj)


def _ssd_kernel(z_ref, x_ref, b_ref, c_ref, dt_ref, cs_ref, cst_ref,
                cwx_ref, cwb_ref, cwc_ref, cbx_ref, cbb_ref, cbc_ref, dsk_ref, nw_ref,
                o_ref, pad_ref, xs_ref, bm_ref, cm_ref, y_ref, h_ref):
    seq = x_ref.shape[1]
    lc = SSM_CHUNK
    n_chunks = seq // lc
    row_tile = 256

    def conv_silu(in_ref, w_ref, bias_ref, out_ref, width):
        zeros = jnp.zeros((CONV_PAD_ROWS, width), F32)
        pad_ref[0:CONV_PAD_ROWS, 0:width] = zeros
        pad_ref[CONV_PAD_ROWS + seq:2 * CONV_PAD_ROWS + seq, 0:width] = zeros
        pad_ref[CONV_PAD_ROWS:CONV_PAD_ROWS + seq, 0:width] = in_ref[0]
        half = (CONV_WIDTH - 1) // 2
        for t in range(seq // row_tile):
            acc = jnp.broadcast_to(bias_ref[...], (row_tile, width))
            for j in range(CONV_WIDTH):
                start = CONV_PAD_ROWS + t * row_tile + j - half
                acc = acc + pad_ref[start:start + row_tile, 0:width] * w_ref[j:j + 1, :]
            out_ref[t * row_tile:(t + 1) * row_tile, :] = _silu(acc)

    conv_silu(x_ref, cwx_ref, cbx_ref, xs_ref, GROUP_W)
    conv_silu(b_ref, cwb_ref, cbb_ref, bm_ref, SSM_STATE)
    conv_silu(c_ref, cwc_ref, cbc_ref, cm_ref, SSM_STATE)

    head_of_lane = lax.broadcasted_iota(jnp.int32, (1, GROUP_W), 1) // SSM_HEAD_DIM
    row = lax.broadcasted_iota(jnp.int32, (lc, lc), 0)
    col = lax.broadcasted_iota(jnp.int32, (lc, lc), 1)

    def expand(cols, off):
        out = cols[:, off + HEADS_PER_GROUP - 1:off + HEADS_PER_GROUP]
        for r in range(HEADS_PER_GROUP - 2, -1, -1):
            out = jnp.where(head_of_lane == r, cols[:, off + r:off + r + 1], out)
        return out

    def run_direction(reverse):
        off = HEADS_PER_GROUP if reverse else 0
        mask = (row <= col) if reverse else (row >= col)
        edge = 0 if reverse else lc - 1
        h_ref[...] = jnp.zeros_like(h_ref)

        def body(ci, carry):
            c = (n_chunks - 1 - ci) if reverse else ci
            r0 = pl.multiple_of(c * lc, lc)
            xc = xs_ref[pl.ds(r0, lc), :]
            bc = bm_ref[pl.ds(r0, lc), :]
            cc = cm_ref[pl.ds(r0, lc), :].astype(BF16)
            dtc = dt_ref[0, 0, pl.ds(r0, lc), :]
            csc = cs_ref[0, 0, pl.ds(r0, lc), :]
            cst = cst_ref[0, 0, c]
            cs_e = expand(csc, off)
            edge_e = expand(csc[edge:edge + 1, :], off)
            xdt = xc * expand(dtc, off)
            xdt_b = xdt.astype(BF16)
            cb = lax.dot_general(cc, bc.astype(BF16), (((1,), (1,)), ((), ())),
                                 preferred_element_type=F32)
            y = jnp.zeros((lc, GROUP_W), F32)
            for r in range(HEADS_PER_GROUP):
                diff = csc[:, off + r:off + r + 1] - cst[off + r:off + r + 1, :]
                decay = jnp.exp(jnp.where(mask, diff, -jnp.inf))
                yr = jnp.dot((cb * decay).astype(BF16), xdt_b, preferred_element_type=F32)
                y = jnp.where(head_of_lane == r, yr, y)
            h_t = h_ref[...]
            y = y + jnp.dot(cc, h_t.astype(BF16), preferred_element_type=F32) * jnp.exp(cs_e)
            new_state = jnp.dot(bc.T.astype(BF16), (xdt * jnp.exp(edge_e - cs_e)).astype(BF16),
                                preferred_element_type=F32)
            h_ref[...] = h_t * jnp.exp(edge_e) + new_state
            if reverse:
                y_ref[pl.ds(r0, lc), :] += y
            else:
                y_ref[pl.ds(r0, lc), :] = y
            return carry

        lax.fori_loop(0, n_chunks, body, 0, unroll=2)

    run_direction(False)
    run_direction(True)

    for t in range(seq // row_tile):
        rows = slice(t * row_tile, (t + 1) * row_tile)
        y = y_ref[rows, :] + xs_ref[rows, :] * dsk_ref[...]
        y = y * _silu(z_ref[0, rows, :])
        y = y * lax.rsqrt(jnp.mean(y * y, axis=-1, keepdims=True) + 1e-6) * nw_ref[...]
        o_ref[0, rows, :] = y.astype(o_ref.dtype)


def _ssd(proj, dtg, csg, cstg, conv_w, conv_b, dskip_e, norm_w, col0_z, col0_xbc):
    bsz, seq, _ = proj.shape
    d_ssm = SSM_HEADS * SSM_HEAD_DIM
    gn = SSM_GROUPS * SSM_STATE
    n_chunks = seq // SSM_CHUNK
    zb, xb = col0_z // GROUP_W, col0_xbc // GROUP_W
    bb, cb = (col0_xbc + d_ssm) // SSM_STATE, (col0_xbc + d_ssm + gn) // SSM_STATE
    wb, wc = d_ssm // SSM_STATE, (d_ssm + gn) // SSM_STATE
    n_dir_heads = 2 * HEADS_PER_GROUP
    return pl.pallas_call(
        _ssd_kernel,
        grid=(bsz, SSM_GROUPS),
        in_specs=[pl.BlockSpec((1, seq, GROUP_W), lambda b, g: (b, 0, zb + g)),
                  pl.BlockSpec((1, seq, GROUP_W), lambda b, g: (b, 0, xb + g)),
                  pl.BlockSpec((1, seq, SSM_STATE), lambda b, g: (b, 0, bb + g)),
                  pl.BlockSpec((1, seq, SSM_STATE), lambda b, g: (b, 0, cb + g)),
                  pl.BlockSpec((1, 1, seq, n_dir_heads), lambda b, g: (b, g, 0, 0)),
                  pl.BlockSpec((1, 1, seq, n_dir_heads), lambda b, g: (b, g, 0, 0)),
                  pl.BlockSpec((1, 1, n_chunks, n_dir_heads, SSM_CHUNK), lambda b, g: (b, g, 0, 0, 0)),
                  pl.BlockSpec((CONV_WIDTH, GROUP_W), lambda b, g: (0, g)),
                  pl.BlockSpec((CONV_WIDTH, SSM_STATE), lambda b, g: (0, wb + g)),
                  pl.BlockSpec((CONV_WIDTH, SSM_STATE), lambda b, g: (0, wc + g)),
                  pl.BlockSpec((1, GROUP_W), lambda b, g: (0, g)),
                  pl.BlockSpec((1, SSM_STATE), lambda b, g: (0, wb + g)),
                  pl.BlockSpec((1, SSM_STATE), lambda b, g: (0, wc + g)),
                  pl.BlockSpec((1, GROUP_W), lambda b, g: (0, g)),
                  pl.BlockSpec((1, GROUP_W), lambda b, g: (0, g))],
        out_specs=pl.BlockSpec((1, seq, GROUP_W), lambda b, g: (b, 0, g)),
        out_shape=jax.ShapeDtypeStruct((bsz, seq, d_ssm), BF16),
        scratch_shapes=[pltpu.VMEM((seq + 2 * CONV_PAD_ROWS, GROUP_W), F32),
                        pltpu.VMEM((seq, GROUP_W), F32),
                        pltpu.VMEM((seq, SSM_STATE), F32),
                        pltpu.VMEM((seq, SSM_STATE), F32),
                        pltpu.VMEM((seq, GROUP_W), F32),
                        pltpu.VMEM((SSM_STATE, GROUP_W), F32)],
        compiler_params=_cparams(("arbitrary", "arbitrary")),
        name="ssd",
    )(proj, proj, proj, proj, dtg, csg, cstg, conv_w, conv_w, conv_w, conv_b, conv_b, conv_b,
      dskip_e, norm_w)


def _merge_kernel(x_ref, wga_ref, wgs_ref, bga_ref, bgs_ref, ya_ref, ys_ref, wba_ref, wbs_ref, u_ref):
    x = x_ref[...]
    ga = _sigmoid(jnp.dot(x, wga_ref[...], preferred_element_type=F32) + bga_ref[...])
    gs = _sigmoid(jnp.dot(x, wgs_ref[...], preferred_element_type=F32) + bgs_ref[...])
    ba = jnp.dot(ya_ref[...], wba_ref[...], preferred_element_type=F32)
    bs = jnp.dot(ys_ref[...], wbs_ref[...], preferred_element_type=F32)
    u_ref[...] = (ga * ba + gs * bs).astype(u_ref.dtype)


def _merge(xb, w_gate, b_gate, y_attn, y_ssm, w_branch, tm, tn):
    m, d = xb.shape
    n = w_branch.shape[1]
    ka, ks = y_attn.shape[1], y_ssm.shape[1]
    nj = n // tn
    return pl.pallas_call(
        _merge_kernel,
        grid=(nj, m // tm),
        in_specs=[pl.BlockSpec((tm, d), lambda j, i: (i, 0)),
                  pl.BlockSpec((d, tn), lambda j, i: (0, j)),
                  pl.BlockSpec((d, tn), lambda j, i: (0, nj + j)),
                  pl.BlockSpec((1, tn), lambda j, i: (0, j)),
                  pl.BlockSpec((1, tn), lambda j, i: (0, nj + j)),
                  pl.BlockSpec((tm, ka), lambda j, i: (i, 0)),
                  pl.BlockSpec((tm, ks), lambda j, i: (i, 0)),
                  pl.BlockSpec((ka, tn), lambda j, i: (0, j)),
                  pl.BlockSpec((ks, tn), lambda j, i: (ka // ks, j))],
        out_specs=pl.BlockSpec((tm, tn), lambda j, i: (i, j)),
        out_shape=jax.ShapeDtypeStruct((m, n), BF16),
        compiler_params=_cparams(("arbitrary", "arbitrary")),
        name="gated_merge",
    )(xb, w_gate, w_gate, b_gate, b_gate, y_attn, y_ssm, w_branch, w_branch)


def _outproj_kernel(u_ref, w_ref, x_ref, g_ref, b_ref, wr_ref, x1_ref, x1b_ref, lg_ref):
    j = pl.program_id(1)
    tn = w_ref.shape[1]
    n_slabs = x1_ref.shape[1] // tn
    part = jnp.dot(u_ref[...], w_ref[...], preferred_element_type=F32)

    for slab in range(n_slabs):
        @pl.when(j == slab)
        def _(slab=slab):
            x1_ref[:, slab * tn:(slab + 1) * tn] = part

    @pl.when(j == n_slabs - 1)
    def _():
        x1 = _layer_norm(ALPHA * x_ref[...] + x1_ref[...], g_ref[...], b_ref[...])
        x1_ref[...] = x1
        x1b_ref[...] = x1.astype(BF16)
        lg_ref[...] = jnp.dot(x1, wr_ref[...], preferred_element_type=F32,
                              precision=lax.Precision.HIGHEST)


def _outproj_ln(u, w_out, x, g, b, w_router_p, tm, tn):
    m, kdim = u.shape
    d = w_out.shape[1]
    row = pl.BlockSpec((tm, d), lambda i, j: (i, 0))
    par = pl.BlockSpec((1, d), lambda i, j: (0, 0))
    return pl.pallas_call(
        _outproj_kernel,
        grid=(m // tm, d // tn),
        in_specs=[pl.BlockSpec((tm, kdim), lambda i, j: (i, 0)),
                  pl.BlockSpec((kdim, tn), lambda i, j: (0, j)),
                  pl.BlockSpec((tm, d), lambda i, j: (i, 0), pipeline_mode=pl.Buffered(1)),
                  par, par,
                  pl.BlockSpec((d, LANES), lambda i, j: (0, 0), pipeline_mode=pl.Buffered(1))],
        out_specs=[row, row, pl.BlockSpec((tm, LANES), lambda i, j: (i, 0))],
        out_shape=[jax.ShapeDtypeStruct((m, d), F32), jax.ShapeDtypeStruct((m, d), BF16),
                   jax.ShapeDtypeStruct((m, LANES), F32)],
        compiler_params=_cparams(("arbitrary", "arbitrary")),
        name="out_proj_ln1",
    )(u, w_out, x, g, b, w_router_p)


def _routing_kernel(lg_ref, slot_ref, slott_ref, gslot_ref, aff_ref, *, cap):
    seq = lg_ref.shape[1]
    blk = 256
    lg = lg_ref[0]
    valid = lax.broadcasted_iota(jnp.int32, lg.shape, 1) < N_EXPERTS
    lgm = jnp.where(valid, lg, -jnp.inf)
    ex = jnp.exp(lgm - jnp.max(lgm, axis=-1, keepdims=True))
    aff = ex / jnp.sum(ex, axis=-1, keepdims=True)
    aff_t = aff.T[0:N_EXPERTS]
    aff_ref[...] = aff_t
    bits = lax.bitcast_convert_type(aff_t, jnp.int32)

    def count(m):
        return jnp.sum(jnp.where(m, 1.0, 0.0), axis=-1, keepdims=True)

    def search(i, thr):
        cand = thr | jnp.left_shift(jnp.int32(1), 30 - i)
        return jnp.where(count(bits >= cand) >= cap, cand, thr)

    thr = lax.fori_loop(0, 31, search, jnp.zeros((N_EXPERTS, 1), jnp.int32))
    gt = bits > thr
    eq = bits == thr

    r_i = lax.broadcasted_iota(jnp.int32, (blk, blk), 0)
    c_i = lax.broadcasted_iota(jnp.int32, (blk, blk), 1)
    before = jnp.where(r_i < c_i, 1.0, 0.0).astype(BF16)

    def excl_cumsum(m):
        mf = jnp.where(m, 1.0, 0.0)
        carry = jnp.zeros((N_EXPERTS, 1), F32)
        parts = []
        for k in range(seq // blk):
            piece = mf[:, k * blk:(k + 1) * blk]
            parts.append(jnp.dot(piece.astype(BF16), before, preferred_element_type=F32) + carry)
            carry = carry + jnp.sum(piece, axis=-1, keepdims=True)
        return jnp.concatenate(parts, axis=-1)

    need = cap - count(gt)
    sel = gt | (eq & (excl_cumsum(eq) < need))
    slot = jnp.where(sel, excl_cumsum(sel), -1.0)
    slot_ref[0] = slot
    pad = jnp.full((LANES - N_EXPERTS, seq), -1.0, F32)
    slott_ref[0] = jnp.concatenate([slot, pad], axis=0).T

    j_iota = lax.broadcasted_iota(jnp.int32, (cap, seq), 0).astype(F32)

    def gate_of_slot(e, carry):
        hit = slot_ref[0, pl.ds(e, 1), :] == j_iota
        gslot_ref[0, e] = jnp.sum(jnp.where(hit, aff_ref[pl.ds(e, 1), :], 0.0), axis=-1, keepdims=True)
        return carry

    lax.fori_loop(0, N_EXPERTS, gate_of_slot, 0)


def _routing(logits, cap):
    bsz, seq, _ = logits.shape
    return pl.pallas_call(
        functools.partial(_routing_kernel, cap=cap),
        grid=(bsz,),
        in_specs=[pl.BlockSpec((1, seq, LANES), lambda b: (b, 0, 0))],
        out_specs=[pl.BlockSpec((1, N_EXPERTS, seq), lambda b: (b, 0, 0)),
                   pl.BlockSpec((1, seq, LANES), lambda b: (b, 0, 0)),
                   pl.BlockSpec((1, N_EXPERTS, cap, 1), lambda b: (b, 0, 0, 0))],
        out_shape=[jax.ShapeDtypeStruct((bsz, N_EXPERTS, seq), F32),
                   jax.ShapeDtypeStruct((bsz, seq, LANES), F32),
                   jax.ShapeDtypeStruct((bsz, N_EXPERTS, cap, 1), F32)],
        scratch_shapes=[pltpu.VMEM((N_EXPERTS, seq), F32)],
        compiler_params=_cparams(("arbitrary",)),
        name="routing",
    )(logits)


def _gather_kernel(slot_ref, x_ref, o_ref, *, cap):
    e = pl.program_id(2)
    seq = x_ref.shape[1]
    j_iota = lax.broadcasted_iota(jnp.int32, (cap, seq), 0).astype(F32)
    pick = jnp.where(slot_ref[0, pl.ds(e, 1), :] == j_iota, 1.0, 0.0).astype(BF16)
    o_ref[0] = jnp.dot(pick, x_ref[0], preferred_element_type=F32).astype(o_ref.dtype)


def _gather(slot, x1b, cap, td):
    bsz, seq, d = x1b.shape
    return pl.pallas_call(
        functools.partial(_gather_kernel, cap=cap),
        grid=(bsz, d // td, N_EXPERTS),
        in_specs=[pl.BlockSpec((1, N_EXPERTS, seq), lambda b, j, e: (b, 0, 0)),
                  pl.BlockSpec((1, seq, td), lambda b, j, e: (b, 0, j))],
        out_specs=pl.BlockSpec((1, cap, td), lambda b, j, e: (e, b, j)),
        out_shape=jax.ShapeDtypeStruct((N_EXPERTS, bsz * cap, d), BF16),
        compiler_params=_cparams(("arbitrary", "arbitrary", "arbitrary")),
        name="moe_gather",
    )(slot, x1b)


def _gateup_kernel(xg_ref, wg_ref, wu_ref, h_ref):
    xg = xg_ref[0]
    g = jnp.dot(xg, wg_ref[0].astype(BF16), preferred_element_type=F32)
    u = jnp.dot(xg, wu_ref[0].astype(BF16), preferred_element_type=F32)
    h_ref[0] = (_silu(g) * u).astype(h_ref.dtype)


def _gateup(xg, w_gate_e, w_up_e, tf):
    n_e, rows, d = xg.shape
    ff = w_gate_e.shape[2]
    wspec = pl.BlockSpec((1, d, tf), lambda e, f: (e, 0, f))
    return pl.pallas_call(
        _gateup_kernel,
        grid=(n_e, ff // tf),
        in_specs=[pl.BlockSpec((1, rows, d), lambda e, f: (e, 0, 0)), wspec, wspec],
        out_specs=pl.BlockSpec((1, rows, tf), lambda e, f: (e, 0, f)),
        out_shape=jax.ShapeDtypeStruct((n_e, rows, ff), BF16),
        compiler_params=_cparams(("arbitrary", "arbitrary")),
        name="moe_gate_up",
    )(xg, w_gate_e, w_up_e)


def _down_kernel(h_ref, wd_ref, gs_ref, y_ref):
    y = jnp.dot(h_ref[0], wd_ref[0].astype(BF16), preferred_element_type=F32)
    y_ref[0] = (y * gs_ref[0]).astype(y_ref.dtype)


def _down(h, w_down_e, gslot, td):
    n_e, rows, ff = h.shape
    d = w_down_e.shape[2]
    return pl.pallas_call(
        _down_kernel,
        grid=(n_e, d // td),
        in_specs=[pl.BlockSpec((1, rows, ff), lambda e, j: (e, 0, 0)),
                  pl.BlockSpec((1, ff, td), lambda e, j: (e, 0, j)),
                  pl.BlockSpec((1, rows, 1), lambda e, j: (e, 0, 0))],
        out_specs=pl.BlockSpec((1, rows, td), lambda e, j: (e, 0, j)),
        out_shape=jax.ShapeDtypeStruct((n_e, rows, d), BF16),
        compiler_params=_cparams(("arbitrary", "arbitrary")),
        name="moe_down",
    )(h, w_down_e, gslot)


def _scatter_kernel(slott_ref, yg_ref, x1_ref, g_ref, b_ref, o_ref, *, cap, ek):
    k = pl.program_id(2)
    ts = o_ref.shape[1]
    st = slott_ref[0]
    lane = lax.broadcasted_iota(jnp.int32, st.shape, 1)
    j_iota = lax.broadcasted_iota(jnp.int32, (ts, cap), 1).astype(F32)
    puts = []
    for i in range(ek):
        col = jnp.sum(jnp.where(lane == k * ek + i, st, 0.0), axis=-1, keepdims=True)
        puts.append(jnp.where(col == j_iota, 1.0, 0.0).astype(BF16))
    acc = jnp.dot(jnp.concatenate(puts, axis=-1), yg_ref[...].reshape(ek * cap, yg_ref.shape[2]),
                  preferred_element_type=F32)

    @pl.when(k == 0)
    def _():
        o_ref[0] = acc

    @pl.when(k > 0)
    def _():
        o_ref[0] += acc

    @pl.when(k == pl.num_programs(2) - 1)
    def _():
        o_ref[0] = _layer_norm(ALPHA * x1_ref[0] + o_ref[0], g_ref[...], b_ref[...])


def _scatter_ln(slot_t, yg, x1, g, b, cap, ts, ek):
    bsz, seq, d = x1.shape
    par = pl.BlockSpec((1, d), lambda bi, i, k: (0, 0))
    return pl.pallas_call(
        functools.partial(_scatter_kernel, cap=cap, ek=ek),
        grid=(bsz, seq // ts, N_EXPERTS // ek),
        in_specs=[pl.BlockSpec((1, ts, LANES), lambda bi, i, k: (bi, i, 0)),
                  pl.BlockSpec((ek, cap, d), lambda bi, i, k: (k, bi, 0)),
                  pl.BlockSpec((1, ts, d), lambda bi, i, k: (bi, i, 0), pipeline_mode=pl.Buffered(1)),
                  par, par],
        out_specs=pl.BlockSpec((1, ts, d), lambda bi, i, k: (bi, i, 0)),
        out_shape=jax.ShapeDtypeStruct((bsz, seq, d), F32),
        compiler_params=_cparams(("arbitrary", "arbitrary", "arbitrary")),
        name="moe_scatter_ln2",
    )(slot_t, yg, x1, g, b)


def _group_heads(t):
    bsz, seq, _ = t.shape
    t = t[:, :, :2 * SSM_HEADS].reshape(bsz, seq, 2, SSM_GROUPS, HEADS_PER_GROUP)
    return jnp.transpose(t, (0, 3, 1, 2, 4)).reshape(bsz, SSM_GROUPS, seq, 2 * HEADS_PER_GROUP)


def _layer(x, w_in, b_gate, lq1, lk1, lq2, lk2, subln_w, conv_w, conv_b, dtb_f, dtb_b, alog_f, alog_b,
           d_skip, ssm_norm_w, w_branch, w_out, ln1_g, ln1_b, w_router, w_gate_e, w_up_e, w_down_e,
           ln2_g, ln2_b, layer_idx):
    bsz, seq, d = x.shape
    m = bsz * seq
    d_qk = ATTN_HEADS * 2 * ATTN_HEAD_DIM
    d_v = ATTN_HEADS * ATTN_V_DIM
    d_ssm = SSM_HEADS * SSM_HEAD_DIM
    d_conv = d_ssm + 2 * SSM_GROUPS * SSM_STATE
    n_main = 2 * d_qk + d_v + d_ssm + d_conv
    n_dt = 2 * SSM_HEADS
    lambda_init = 0.8 - 0.6 * math.exp(-0.3 * layer_idx)
    cap = CAPACITY_FACTOR * seq // N_EXPERTS
    row = lambda v: v.reshape(1, -1)

    xb = x.reshape(m, d).astype(BF16)
    w_dt = jnp.pad(w_in[:, n_main:n_main + n_dt], ((0, 0), (0, LANES - n_dt))).astype(BF16)
    w_gate = w_in[:, n_main + n_dt:].astype(BF16)

    proj = _matmul_f32w(xb, w_in, n_main, F32, 1024, 512).reshape(bsz, seq, n_main)
    dt_raw = _matmul(xb, w_dt, F32, 512, LANES).reshape(bsz, seq, LANES)

    lane_pad = lambda a, bvec: jnp.pad(jnp.concatenate([a, bvec]), (0, LANES - n_dt)).reshape(1, LANES)
    dt, cs = _dtprep(dt_raw, lane_pad(dtb_f, dtb_b), lane_pad(alog_f, alog_b))
    dtg, csg = _group_heads(dt), _group_heads(cs)
    n_chunks = seq // SSM_CHUNK
    cstg = jnp.transpose(csg.reshape(bsz, SSM_GROUPS, n_chunks, SSM_CHUNK, 2 * HEADS_PER_GROUP),
                         (0, 1, 2, 4, 3))

    y_attn = _attention(proj, row(lq1), row(lk1), row(lq2), row(lk2), row(subln_w), lambda_init, 256)
    y_ssm = _ssd(proj, dtg, csg, cstg, conv_w, row(conv_b), row(jnp.repeat(d_skip, SSM_HEAD_DIM)),
                 row(ssm_norm_w), 2 * d_qk + d_v, 2 * d_qk + d_v + d_ssm)

    u = _merge(xb, w_gate, row(b_gate), y_attn.reshape(m, d_v), y_ssm.reshape(m, d_ssm),
               w_branch.astype(BF16), 512, 512)
    w_router_p = jnp.pad(w_router, ((0, 0), (0, LANES - N_EXPERTS)))
    x1, x1b, logits = _outproj_ln(u, w_out.astype(BF16), x.reshape(m, d), row(ln1_g), row(ln1_b),
                                  w_router_p, 512, 256)

    slot, slot_t, gslot = _routing(logits.reshape(bsz, seq, LANES), cap)
    xg = _gather(slot, x1b.reshape(bsz, seq, d), cap, 1024)
    h = _gateup(xg, w_gate_e, w_up_e, 256)
    gslot_e = jnp.transpose(gslot, (1, 0, 2, 3)).reshape(N_EXPERTS, bsz * cap, 1)
    yg = _down(h, w_down_e, gslot_e, 512)
    return _scatter_ln(slot_t, yg, x1.reshape(bsz, seq, d), row(ln2_g), row(ln2_b), cap, 512, 2)


def kernel(x, w_in, b_gate, lambda_q1, lambda_k1, lambda_q2, lambda_k2, attn_subln_w, conv_w, conv_b,
           dt_bias_fwd, dt_bias_bwd, a_log_fwd, a_log_bwd, d_skip, ssm_norm_w, w_branch, w_out,
           ln1_g, ln1_b, w_router, w_gate_e, w_up_e, w_down_e, ln2_g, ln2_b):
    for l in range(w_in.shape[0]):
        x = _layer(x, w_in[l], b_gate[l], lambda_q1[l], lambda_k1[l], lambda_q2[l], lambda_k2[l],
                   attn_subln_w[l], conv_w[l], conv_b[l], dt_bias_fwd[l], dt_bias_bwd[l],
                   a_log_fwd[l], a_log_bwd[l], d_skip[l], ssm_norm_w[l], w_branch[l], w_out[l],
                   ln1_g[l], ln1_b[l], w_router[l], w_gate_e[l], w_up_e[l], w_down_e[l],
                   ln2_g[l], ln2_b[l], l)
    return x
```

```python
import functools
import math

import jax
import jax.numpy as jnp
from jax import lax
from jax.experimental import pallas as pl
from jax.experimental.pallas import tpu as pltpu

F32 = jnp.float32
BF16 = jnp.bfloat16

ATTN_HEADS = 16
ATTN_HEAD_DIM = 64
ATTN_V_DIM = 128
ROPE_THETA = 10000.0
LOG2_E = math.log2(math.e)
SSM_HEAD_DIM = 64
SSM_HEADS = 32
SSM_GROUPS = 8
SSM_STATE = 128
SSM_CHUNK = 128
CONV_WIDTH = 5
N_EXPERTS = 16
CAPACITY_FACTOR = 2
DEPTH = 1
ALPHA = (2.0 * DEPTH) ** 0.25

LANES = 128
SUBLANES = 8
VMEM_LIMIT = 56 * 1024 * 1024

HEADS_PER_GROUP = SSM_HEADS // SSM_GROUPS
GROUP_W = HEADS_PER_GROUP * SSM_HEAD_DIM
CONV_PAD_ROWS = SUBLANES


def _cparams(sem):
    return pltpu.CompilerParams(dimension_semantics=sem, vmem_limit_bytes=VMEM_LIMIT)


def _sigmoid(x):
    return 1.0 / (1.0 + jnp.exp(-x))


def _silu(x):
    return x * _sigmoid(x)


def _softplus(x):
    return jnp.maximum(x, 0.0) + jnp.log1p(jnp.exp(-jnp.abs(x)))


def _layer_norm(r, g, b):
    mu = jnp.mean(r, axis=-1, keepdims=True)
    d = r - mu
    var = jnp.mean(d * d, axis=-1, keepdims=True)
    return d * lax.rsqrt(var + 1e-5) * g + b


def _mm_wt_kernel(a_ref, w_ref, o_ref, wb_ref):
    @pl.when(pl.program_id(1) == 0)
    def _():
        wb_ref[...] = w_ref[...].astype(BF16)

    o_ref[...] = lax.dot_general(a_ref[...], wb_ref[...], (((1,), (1,)), ((), ())),
                                 preferred_element_type=F32).astype(o_ref.dtype)


def _matmul_wt(a, w_t, blk0, n_blk, out_dtype, tm, tn, name):
    m, k = a.shape
    return pl.pallas_call(
        _mm_wt_kernel,
        grid=(n_blk, m // tm),
        in_specs=[pl.BlockSpec((tm, k), lambda j, i: (i, 0)),
                  pl.BlockSpec((tn, k), lambda j, i: (blk0 + j, 0))],
        out_specs=pl.BlockSpec((tm, tn), lambda j, i: (i, j)),
        out_shape=jax.ShapeDtypeStruct((m, n_blk * tn), out_dtype),
        scratch_shapes=[pltpu.VMEM((tn, k), BF16)],
        compiler_params=_cparams(("arbitrary", "arbitrary")),
        name=name,
    )(a, w_t)


def _dtprep_kernel(raw_ref, bias_ref, alog_ref, dt_ref, cs_ref):
    seq = raw_ref.shape[1]
    lc = SSM_CHUNK
    dt = _softplus(raw_ref[0] + bias_ref[...])
    dt_ref[0] = dt
    la = dt * (-jnp.exp(alog_ref[...]))
    row = lax.broadcasted_iota(jnp.int32, (lc, lc), 0)
    col = lax.broadcasted_iota(jnp.int32, (lc, lc), 1)
    t_low = jnp.where(row >= col, 1.0, 0.0).astype(F32)
    t_up = jnp.where(row <= col, 1.0, 0.0).astype(F32)
    fwd_lane = lax.broadcasted_iota(jnp.int32, (1, LANES), 1) < SSM_HEADS
    for c in range(seq // lc):
        lac = la[c * lc:(c + 1) * lc]
        f = jnp.dot(t_low, lac, preferred_element_type=F32, precision=lax.Precision.HIGHEST)
        b = jnp.dot(t_up, lac, preferred_element_type=F32, precision=lax.Precision.HIGHEST)
        cs_ref[0, c * lc:(c + 1) * lc, :] = jnp.where(fwd_lane, f, b)


def _dtprep(raw, bias, alog):
    bsz, seq, _ = raw.shape
    blk = pl.BlockSpec((1, seq, LANES), lambda b: (b, 0, 0))
    par = pl.BlockSpec((1, LANES), lambda b: (0, 0))
    return pl.pallas_call(
        _dtprep_kernel,
        grid=(bsz,),
        in_specs=[blk, par, par],
        out_specs=[blk, blk],
        out_shape=[jax.ShapeDtypeStruct(raw.shape, F32)] * 2,
        compiler_params=_cparams(("arbitrary",)),
        name="dt_prep",
    )(raw, bias, alog)


def _attn_kernel(lq1_ref, lk1_ref, lq2_ref, lk2_ref, sw_ref, cq_ref, sq_ref, ck_ref, sk_ref,
                 q_ref, k_ref, v_ref, o_ref, kr_ref, vb_ref, *, lambda_init):
    tq = q_ref.shape[1]
    lane = lax.broadcasted_iota(jnp.int32, (1, LANES), 1)
    first_half = (lane & (ATTN_HEAD_DIM // 2)) == 0
    comp1 = lane < ATTN_HEAD_DIM

    def rope(x, c, s):
        partner = jnp.where(first_half,
                            pltpu.roll(x, LANES - ATTN_HEAD_DIM // 2, 1),
                            pltpu.roll(x, ATTN_HEAD_DIM // 2, 1))
        return x * c + partner * s

    @pl.when(pl.program_id(2) == 0)
    def _():
        kr_ref[...] = rope(k_ref[0], ck_ref[...], sk_ref[...]).astype(BF16)
        vb_ref[...] = v_ref[0].astype(BF16)

    lam = (jnp.exp(jnp.sum(lq1_ref[...] * lk1_ref[...], axis=-1, keepdims=True))
           - jnp.exp(jnp.sum(lq2_ref[...] * lk2_ref[...], axis=-1, keepdims=True)) + lambda_init)

    q = rope(q_ref[0], cq_ref[...], sq_ref[...]) * (ATTN_HEAD_DIM ** -0.5 * LOG2_E)
    qs = jnp.concatenate([jnp.where(comp1, q, 0.0), jnp.where(comp1, 0.0, q)], axis=0).astype(BF16)
    s = lax.dot_general(qs, kr_ref[...], (((1,), (1,)), ((), ())), preferred_element_type=F32)
    p = jnp.exp2(s - jnp.max(s, axis=-1, keepdims=True))
    inv = 1.0 / jnp.sum(p, axis=-1, keepdims=True)
    pc = (p[:tq] * inv[:tq] - p[tq:] * (lam * inv[tq:])).astype(BF16)
    o = jnp.dot(pc, vb_ref[...], preferred_element_type=F32)
    o = o * lax.rsqrt(jnp.mean(o * o, axis=-1, keepdims=True) + 1e-6) * sw_ref[...]
    o_ref[0] = (o * (1.0 - lambda_init)).astype(o_ref.dtype)


def _rope_tables(seq):
    half = ATTN_HEAD_DIM // 2
    inv_freq = ROPE_THETA ** (-jnp.arange(0, ATTN_HEAD_DIM, 2, dtype=F32) / ATTN_HEAD_DIM)
    ang = jnp.arange(seq, dtype=F32)[:, None] * inv_freq[None, :]
    cos, sin = jnp.cos(ang), jnp.sin(ang)
    reps = LANES // half
    sign = jnp.tile(jnp.concatenate([-jnp.ones((half,), F32), jnp.ones((half,), F32)]), reps // 2)
    return jnp.tile(cos, (1, reps)), jnp.tile(sin, (1, reps)) * sign[None, :]


def _attention(proj, lq1, lk1, lq2, lk2, subln_w, lambda_init, tq):
    bsz, seq, _ = proj.shape
    cos, sin = _rope_tables(seq)
    vec = pl.BlockSpec((1, ATTN_HEAD_DIM), lambda b, h, i: (0, 0))
    return pl.pallas_call(
        functools.partial(_attn_kernel, lambda_init=lambda_init),
        grid=(bsz, ATTN_HEADS, seq // tq),
        in_specs=[vec, vec, vec, vec,
                  pl.BlockSpec((1, LANES), lambda b, h, i: (0, 0)),
                  pl.BlockSpec((tq, LANES), lambda b, h, i: (i, 0)),
                  pl.BlockSpec((tq, LANES), lambda b, h, i: (i, 0)),
                  pl.BlockSpec((seq, LANES), lambda b, h, i: (0, 0)),
                  pl.BlockSpec((seq, LANES), lambda b, h, i: (0, 0)),
                  pl.BlockSpec((1, tq, LANES), lambda b, h, i: (b, i, h)),
                  pl.BlockSpec((1, seq, LANES), lambda b, h, i: (b, 0, ATTN_HEADS + h)),
                  pl.BlockSpec((1, seq, LANES), lambda b, h, i: (b, 0, 2 * ATTN_HEADS + h))],
        out_specs=pl.BlockSpec((1, tq, LANES), lambda b, h, i: (b, i, h)),
        out_shape=jax.ShapeDtypeStruct((bsz, seq, ATTN_HEADS * ATTN_V_DIM), BF16),
        scratch_shapes=[pltpu.VMEM((seq, LANES), BF16), pltpu.VMEM((seq, ATTN_V_DIM), BF16)],
        compiler_params=_cparams(("arbitrary", "arbitrary", "arbitrary")),
        name="diff_attention",
    )(lq1, lk1, lq2, lk2, subln_w, cos, sin, cos, sin, proj, proj, proj)


def _ssd_kernel(z_ref, x_ref, b_ref, c_ref, dt_ref, cs_ref, cst_ref,
                cwx_ref, cwb_ref, cwc_ref, cbx_ref, cbb_ref, cbc_ref, dsk_ref, nw_ref,
                o_ref, pad_ref, xs_ref, bm_ref, cm_ref, y_ref, h_ref):
    seq = x_ref.shape[1]
    lc = SSM_CHUNK
    n_chunks = seq // lc
    row_tile = 256

    def conv_silu(in_ref, w_ref, bias_ref, out_ref, width):
        zeros = jnp.zeros((CONV_PAD_ROWS, width), F32)
        pad_ref[0:CONV_PAD_ROWS, 0:width] = zeros
        pad_ref[CONV_PAD_ROWS + seq:2 * CONV_PAD_ROWS + seq, 0:width] = zeros
        pad_ref[CONV_PAD_ROWS:CONV_PAD_ROWS + seq, 0:width] = in_ref[0]
        half = (CONV_WIDTH - 1) // 2
        for t in range(seq // row_tile):
            acc = jnp.broadcast_to(bias_ref[...], (row_tile, width))
            for j in range(CONV_WIDTH):
                start = CONV_PAD_ROWS + t * row_tile + j - half
                acc = acc + pad_ref[start:start + row_tile, 0:width] * w_ref[j:j + 1, :]
            out_ref[t * row_tile:(t + 1) * row_tile, :] = _silu(acc)

    conv_silu(x_ref, cwx_ref, cbx_ref, xs_ref, GROUP_W)
    conv_silu(b_ref, cwb_ref, cbb_ref, bm_ref, SSM_STATE)
    conv_silu(c_ref, cwc_ref, cbc_ref, cm_ref, SSM_STATE)

    head_of_lane = lax.broadcasted_iota(jnp.int32, (1, GROUP_W), 1) // SSM_HEAD_DIM
    row = lax.broadcasted_iota(jnp.int32, (lc, lc), 0)
    col = lax.broadcasted_iota(jnp.int32, (lc, lc), 1)

    def expand(cols, off):
        out = cols[:, off + HEADS_PER_GROUP - 1:off + HEADS_PER_GROUP]
        for r in range(HEADS_PER_GROUP - 2, -1, -1):
            out = jnp.where(head_of_lane == r, cols[:, off + r:off + r + 1], out)
        return out

    def run_direction(reverse):
        off = HEADS_PER_GROUP if reverse else 0
        mask = (row <= col) if reverse else (row >= col)
        edge = 0 if reverse else lc - 1
        h_ref[...] = jnp.zeros_like(h_ref)

        def body(ci, carry):
            c = (n_chunks - 1 - ci) if reverse else ci
            r0 = pl.multiple_of(c * lc, lc)
            xc = xs_ref[pl.ds(r0, lc), :]
            bc = bm_ref[pl.ds(r0, lc), :]
            cc = cm_ref[pl.ds(r0, lc), :].astype(BF16)
            dtc = dt_ref[0, 0, pl.ds(r0, lc), :]
            csc = cs_ref[0, 0, pl.ds(r0, lc), :]
            cst = cst_ref[0, 0, c]
            cs_e = expand(csc, off)
            edge_e = expand(csc[edge:edge + 1, :], off)
            xdt = xc * expand(dtc, off)
            xdt_b = xdt.astype(BF16)
            cb = lax.dot_general(cc, bc.astype(BF16), (((1,), (1,)), ((), ())),
                                 preferred_element_type=F32)
            y = jnp.zeros((lc, GROUP_W), F32)
            for r in range(HEADS_PER_GROUP):
                diff = csc[:, off + r:off + r + 1] - cst[off + r:off + r + 1, :]
                decay = jnp.exp(jnp.where(mask, diff, -jnp.inf))
                yr = jnp.dot((cb * decay).astype(BF16), xdt_b, preferred_element_type=F32)
                y = jnp.where(head_of_lane == r, yr, y)
            h_t = h_ref[...]
            y = y + jnp.dot(cc, h_t.astype(BF16), preferred_element_type=F32) * jnp.exp(cs_e)
            new_state = jnp.dot(bc.T.astype(BF16), (xdt * jnp.exp(edge_e - cs_e)).astype(BF16),
                                preferred_element_type=F32)
            h_ref[...] = h_t * jnp.exp(edge_e) + new_state
            if reverse:
                y_ref[pl.ds(r0, lc), :] += y
            else:
                y_ref[pl.ds(r0, lc), :] = y
            return carry

        lax.fori_loop(0, n_chunks, body, 0, unroll=2)

    run_direction(False)
    run_direction(True)

    for t in range(seq // row_tile):
        rows = slice(t * row_tile, (t + 1) * row_tile)
        y = y_ref[rows, :] + xs_ref[rows, :] * dsk_ref[...]
        y = y * _silu(z_ref[0, rows, :])
        y = y * lax.rsqrt(jnp.mean(y * y, axis=-1, keepdims=True) + 1e-6) * nw_ref[...]
        o_ref[0, rows, :] = y.astype(o_ref.dtype)


def _ssd(proj, dtg, csg, cstg, conv_w, conv_b, dskip_e, norm_w, col0_z, col0_xbc):
    bsz, seq, _ = proj.shape
    d_ssm = SSM_HEADS * SSM_HEAD_DIM
    gn = SSM_GROUPS * SSM_STATE
    n_chunks = seq // SSM_CHUNK
    zb, xb = col0_z // GROUP_W, col0_xbc // GROUP_W
    bb, cb = (col0_xbc + d_ssm) // SSM_STATE, (col0_xbc + d_ssm + gn) // SSM_STATE
    wb, wc = d_ssm // SSM_STATE, (d_ssm + gn) // SSM_STATE
    n_dir_heads = 2 * HEADS_PER_GROUP
    return pl.pallas_call(
        _ssd_kernel,
        grid=(bsz, SSM_GROUPS),
        in_specs=[pl.BlockSpec((1, seq, GROUP_W), lambda b, g: (b, 0, zb + g)),
                  pl.BlockSpec((1, seq, GROUP_W), lambda b, g: (b, 0, xb + g)),
                  pl.BlockSpec((1, seq, SSM_STATE), lambda b, g: (b, 0, bb + g)),
                  pl.BlockSpec((1, seq, SSM_STATE), lambda b, g: (b, 0, cb + g)),
                  pl.BlockSpec((1, 1, seq, n_dir_heads), lambda b, g: (b, g, 0, 0)),
                  pl.BlockSpec((1, 1, seq, n_dir_heads), lambda b, g: (b, g, 0, 0)),
                  pl.BlockSpec((1, 1, n_chunks, n_dir_heads, SSM_CHUNK), lambda b, g: (b, g, 0, 0, 0)),
                  pl.BlockSpec((CONV_WIDTH, GROUP_W), lambda b, g: (0, g)),
                  pl.BlockSpec((CONV_WIDTH, SSM_STATE), lambda b, g: (0, wb + g)),
                  pl.BlockSpec((CONV_WIDTH, SSM_STATE), lambda b, g: (0, wc + g)),
                  pl.BlockSpec((1, GROUP_W), lambda b, g: (0, g)),
                  pl.BlockSpec((1, SSM_STATE), lambda b, g: (0, wb + g)),
                  pl.BlockSpec((1, SSM_STATE), lambda b, g: (0, wc + g)),
                  pl.BlockSpec((1, GROUP_W), lambda b, g: (0, g)),
                  pl.BlockSpec((1, GROUP_W), lambda b, g: (0, g))],
        out_specs=pl.BlockSpec((1, seq, GROUP_W), lambda b, g: (b, 0, g)),
        out_shape=jax.ShapeDtypeStruct((bsz, seq, d_ssm), BF16),
        scratch_shapes=[pltpu.VMEM((seq + 2 * CONV_PAD_ROWS, GROUP_W), F32),
                        pltpu.VMEM((seq, GROUP_W), F32),
                        pltpu.VMEM((seq, SSM_STATE), F32),
                        pltpu.VMEM((seq, SSM_STATE), F32),
                        pltpu.VMEM((seq, GROUP_W), F32),
                        pltpu.VMEM((SSM_STATE, GROUP_W), F32)],
        compiler_params=_cparams(("arbitrary", "arbitrary")),
        name="ssd",
    )(proj, proj, proj, proj, dtg, csg, cstg, conv_w, conv_w, conv_w, conv_b, conv_b, conv_b,
      dskip_e, norm_w)


def _merge_kernel(x_ref, wga_ref, wgs_ref, bga_ref, bgs_ref, ya_ref, ys_ref, wba_ref, wbs_ref, u_ref):
    x = x_ref[...]
    nt = (((1,), (1,)), ((), ()))
    ga = _sigmoid(lax.dot_general(x, wga_ref[...], nt, preferred_element_type=F32) + bga_ref[...])
    gs = _sigmoid(lax.dot_general(x, wgs_ref[...], nt, preferred_element_type=F32) + bgs_ref[...])
    ba = jnp.dot(ya_ref[...], wba_ref[...], preferred_element_type=F32)
    bs = jnp.dot(ys_ref[...], wbs_ref[...], preferred_element_type=F32)
    u_ref[...] = (ga * ba + gs * bs).astype(u_ref.dtype)


def _merge(xb, w_gate, b_gate, y_attn, y_ssm, w_branch, tm, tn):
    m, d = xb.shape
    n = w_branch.shape[1]
    ka, ks = y_attn.shape[1], y_ssm.shape[1]
    nj = n // tn
    return pl.pallas_call(
        _merge_kernel,
        grid=(nj, m // tm),
        in_specs=[pl.BlockSpec((tm, d), lambda j, i: (i, 0)),
                  pl.BlockSpec((tn, d), lambda j, i: (j, 0)),
                  pl.BlockSpec((tn, d), lambda j, i: (nj + j, 0)),
                  pl.BlockSpec((1, tn), lambda j, i: (0, j)),
                  pl.BlockSpec((1, tn), lambda j, i: (0, nj + j)),
                  pl.BlockSpec((tm, ka), lambda j, i: (i, 0)),
                  pl.BlockSpec((tm, ks), lambda j, i: (i, 0)),
                  pl.BlockSpec((ka, tn), lambda j, i: (0, j)),
                  pl.BlockSpec((ks, tn), lambda j, i: (ka // ks, j))],
        out_specs=pl.BlockSpec((tm, tn), lambda j, i: (i, j)),
        out_shape=jax.ShapeDtypeStruct((m, n), BF16),
        compiler_params=_cparams(("arbitrary", "arbitrary")),
        name="gated_merge",
    )(xb, w_gate, w_gate, b_gate, b_gate, y_attn, y_ssm, w_branch, w_branch)


def _outproj_kernel(u_ref, w_ref, x_ref, g_ref, b_ref, wr_ref, x1_ref, lg_ref):
    j = pl.program_id(1)
    tn = w_ref.shape[1]
    n_slabs = x1_ref.shape[1] // tn
    part = jnp.dot(u_ref[...], w_ref[...], preferred_element_type=F32)

    for slab in range(n_slabs):
        @pl.when(j == slab)
        def _(slab=slab):
            x1_ref[:, slab * tn:(slab + 1) * tn] = part

    @pl.when(j == n_slabs - 1)
    def _():
        x1 = _layer_norm(ALPHA * x_ref[...] + x1_ref[...], g_ref[...], b_ref[...])
        x1_ref[...] = x1
        lg_ref[...] = jnp.dot(x1.astype(BF16), wr_ref[...], preferred_element_type=F32)


def _outproj_ln(u, w_out, x, g, b, w_router_p, tm, tn):
    m, kdim = u.shape
    d = w_out.shape[1]
    row = pl.BlockSpec((tm, d), lambda i, j: (i, 0))
    par = pl.BlockSpec((1, d), lambda i, j: (0, 0))
    return pl.pallas_call(
        _outproj_kernel,
        grid=(m // tm, d // tn),
        in_specs=[pl.BlockSpec((tm, kdim), lambda i, j: (i, 0)),
                  pl.BlockSpec((kdim, tn), lambda i, j: (0, j)),
                  pl.BlockSpec((tm, d), lambda i, j: (i, 0), pipeline_mode=pl.Buffered(1)),
                  par, par,
                  pl.BlockSpec((d, LANES), lambda i, j: (0, 0), pipeline_mode=pl.Buffered(1))],
        out_specs=[row, pl.BlockSpec((tm, LANES), lambda i, j: (i, 0))],
        out_shape=[jax.ShapeDtypeStruct((m, d), F32), jax.ShapeDtypeStruct((m, LANES), F32)],
        compiler_params=_cparams(("arbitrary", "arbitrary")),
        name="out_proj_ln1",
    )(u, w_out, x, g, b, w_router_p)


def _routing_kernel(lg_ref, slot_ref, slott_ref, gslot_ref, aff_ref, *, cap):
    seq = lg_ref.shape[1]
    blk = 256
    lg = lg_ref[0]
    valid = lax.broadcasted_iota(jnp.int32, lg.shape, 1) < N_EXPERTS
    lgm = jnp.where(valid, lg, -jnp.inf)
    ex = jnp.exp(lgm - jnp.max(lgm, axis=-1, keepdims=True))
    aff = ex / jnp.sum(ex, axis=-1, keepdims=True)
    aff_t = aff.T[0:N_EXPERTS]
    aff_ref[...] = aff_t
    bits = lax.bitcast_convert_type(aff_t, jnp.int32)

    def count(m):
        return jnp.sum(jnp.where(m, 1.0, 0.0), axis=-1, keepdims=True)

    def search(i, thr):
        cand = thr | jnp.left_shift(jnp.int32(1), 30 - i)
        return jnp.where(count(bits >= cand) >= cap, cand, thr)

    thr = lax.fori_loop(0, 31, search, jnp.zeros((N_EXPERTS, 1), jnp.int32))
    gt = bits > thr
    eq = bits == thr

    r_i = lax.broadcasted_iota(jnp.int32, (blk, blk), 0)
    c_i = lax.broadcasted_iota(jnp.int32, (blk, blk), 1)
    before = jnp.where(r_i < c_i, 1.0, 0.0).astype(BF16)

    def excl_cumsum(m):
        mf = jnp.where(m, 1.0, 0.0)
        carry = jnp.zeros((N_EXPERTS, 1), F32)
        parts = []
        for k in range(seq // blk):
            piece = mf[:, k * blk:(k + 1) * blk]
            parts.append(jnp.dot(piece.astype(BF16), before, preferred_element_type=F32) + carry)
            carry = carry + jnp.sum(piece, axis=-1, keepdims=True)
        return jnp.concatenate(parts, axis=-1)

    need = cap - count(gt)
    sel = gt | (eq & (excl_cumsum(eq) < need))
    slot = jnp.where(sel, excl_cumsum(sel), -1.0)
    slot_ref[0] = slot
    pad = jnp.full((LANES - N_EXPERTS, seq), -1.0, F32)
    slott_ref[0] = jnp.concatenate([slot, pad], axis=0).T

    j_iota = lax.broadcasted_iota(jnp.int32, (cap, seq), 0).astype(F32)

    def gate_of_slot(e, carry):
        hit = slot_ref[0, pl.ds(e, 1), :] == j_iota
        gslot_ref[0, e] = jnp.sum(jnp.where(hit, aff_ref[pl.ds(e, 1), :], 0.0), axis=-1, keepdims=True)
        return carry

    lax.fori_loop(0, N_EXPERTS, gate_of_slot, 0)


def _routing(logits, cap):
    bsz, seq, _ = logits.shape
    return pl.pallas_call(
        functools.partial(_routing_kernel, cap=cap),
        grid=(bsz,),
        in_specs=[pl.BlockSpec((1, seq, LANES), lambda b: (b, 0, 0))],
        out_specs=[pl.BlockSpec((1, N_EXPERTS, seq), lambda b: (b, 0, 0)),
                   pl.BlockSpec((1, seq, LANES), lambda b: (b, 0, 0)),
                   pl.BlockSpec((1, N_EXPERTS, cap, 1), lambda b: (b, 0, 0, 0))],
        out_shape=[jax.ShapeDtypeStruct((bsz, N_EXPERTS, seq), F32),
                   jax.ShapeDtypeStruct((bsz, seq, LANES), F32),
                   jax.ShapeDtypeStruct((bsz, N_EXPERTS, cap, 1), F32)],
        scratch_shapes=[pltpu.VMEM((N_EXPERTS, seq), F32)],
        compiler_params=_cparams(("arbitrary",)),
        name="routing",
    )(logits)


def _gather_kernel(slot_ref, x_ref, o_ref, *, cap):
    e = pl.program_id(2)
    seq = x_ref.shape[1]
    j_iota = lax.broadcasted_iota(jnp.int32, (cap, seq), 0).astype(F32)
    pick = jnp.where(slot_ref[0, pl.ds(e, 1), :] == j_iota, 1.0, 0.0).astype(BF16)
    o_ref[0] = jnp.dot(pick, x_ref[0].astype(BF16), preferred_element_type=F32).astype(o_ref.dtype)


def _gather(slot, x1b, cap, td):
    bsz, seq, d = x1b.shape
    return pl.pallas_call(
        functools.partial(_gather_kernel, cap=cap),
        grid=(bsz, d // td, N_EXPERTS),
        in_specs=[pl.BlockSpec((1, N_EXPERTS, seq), lambda b, j, e: (b, 0, 0)),
                  pl.BlockSpec((1, seq, td), lambda b, j, e: (b, 0, j))],
        out_specs=pl.BlockSpec((1, cap, td), lambda b, j, e: (e, b, j)),
        out_shape=jax.ShapeDtypeStruct((N_EXPERTS, bsz * cap, d), BF16),
        compiler_params=_cparams(("arbitrary", "arbitrary", "arbitrary")),
        name="moe_gather",
    )(slot, x1b)


def _gateup_kernel(xg_ref, wg_ref, wu_ref, h_ref):
    xg = xg_ref[0]
    g = jnp.dot(xg, wg_ref[0].astype(BF16), preferred_element_type=F32)
    u = jnp.dot(xg, wu_ref[0].astype(BF16), preferred_element_type=F32)
    h_ref[0] = (_silu(g) * u).astype(h_ref.dtype)


def _gateup(xg, w_gate_e, w_up_e, tf):
    n_e, rows, d = xg.shape
    ff = w_gate_e.shape[2]
    wspec = pl.BlockSpec((1, d, tf), lambda e, f: (e, 0, f))
    return pl.pallas_call(
        _gateup_kernel,
        grid=(n_e, ff // tf),
        in_specs=[pl.BlockSpec((1, rows, d), lambda e, f: (e, 0, 0)), wspec, wspec],
        out_specs=pl.BlockSpec((1, rows, tf), lambda e, f: (e, 0, f)),
        out_shape=jax.ShapeDtypeStruct((n_e, rows, ff), BF16),
        compiler_params=_cparams(("arbitrary", "arbitrary")),
        name="moe_gate_up",
    )(xg, w_gate_e, w_up_e)


def _down_kernel(h_ref, wd_ref, gs_ref, y_ref):
    y = jnp.dot(h_ref[0], wd_ref[0].astype(BF16), preferred_element_type=F32)
    y_ref[0] = (y * gs_ref[0]).astype(y_ref.dtype)


def _down(h, w_down_e, gslot, td):
    n_e, rows, ff = h.shape
    d = w_down_e.shape[2]
    return pl.pallas_call(
        _down_kernel,
        grid=(n_e, d // td),
        in_specs=[pl.BlockSpec((1, rows, ff), lambda e, j: (e, 0, 0)),
                  pl.BlockSpec((1, ff, td), lambda e, j: (e, 0, j)),
                  pl.BlockSpec((1, rows, 1), lambda e, j: (e, 0, 0))],
        out_specs=pl.BlockSpec((1, rows, td), lambda e, j: (e, 0, j)),
        out_shape=jax.ShapeDtypeStruct((n_e, rows, d), BF16),
        compiler_params=_cparams(("arbitrary", "arbitrary")),
        name="moe_down",
    )(h, w_down_e, gslot)


def _scatter_kernel(slott_ref, yg_ref, x1_ref, g_ref, b_ref, o_ref, put_ref, *, cap):
    dj = pl.program_id(2)
    ts = o_ref.shape[1]
    td = yg_ref.shape[2]
    n_slabs = o_ref.shape[2] // td

    @pl.when(dj == 0)
    def _():
        st = slott_ref[0]
        lane = lax.broadcasted_iota(jnp.int32, st.shape, 1)
        j_iota = lax.broadcasted_iota(jnp.int32, (ts, cap), 1).astype(F32)
        for e in range(N_EXPERTS):
            col = jnp.sum(jnp.where(lane == e, st, 0.0), axis=-1, keepdims=True)
            put_ref[:, e * cap:(e + 1) * cap] = jnp.where(col == j_iota, 1.0, 0.0).astype(BF16)

    part = jnp.dot(put_ref[...], yg_ref[...].reshape(N_EXPERTS * cap, td), preferred_element_type=F32)

    for slab in range(n_slabs):
        @pl.when(dj == slab)
        def _(slab=slab):
            o_ref[0, :, slab * td:(slab + 1) * td] = part

    @pl.when(dj == n_slabs - 1)
    def _():
        o_ref[0] = _layer_norm(ALPHA * x1_ref[0] + o_ref[0], g_ref[...], b_ref[...])


def _scatter_ln(slot_t, yg, x1, g, b, cap, ts, td):
    bsz, seq, d = x1.shape
    par = pl.BlockSpec((1, d), lambda bi, i, j: (0, 0))
    return pl.pallas_call(
        functools.partial(_scatter_kernel, cap=cap),
        grid=(bsz, seq // ts, d // td),
        in_specs=[pl.BlockSpec((1, ts, LANES), lambda bi, i, j: (bi, i, 0)),
                  pl.BlockSpec((N_EXPERTS, cap, td), lambda bi, i, j: (0, bi, j)),
                  pl.BlockSpec((1, ts, d), lambda bi, i, j: (bi, i, 0), pipeline_mode=pl.Buffered(1)),
                  par, par],
        out_specs=pl.BlockSpec((1, ts, d), lambda bi, i, j: (bi, i, 0)),
        out_shape=jax.ShapeDtypeStruct((bsz, seq, d), F32),
        scratch_shapes=[pltpu.VMEM((ts, N_EXPERTS * cap), BF16)],
        compiler_params=_cparams(("arbitrary", "arbitrary", "arbitrary")),
        name="moe_scatter_ln2",
    )(slot_t, yg, x1, g, b)


def _group_heads(t):
    bsz, seq, _ = t.shape
    t = t[:, :, :2 * SSM_HEADS].reshape(bsz, seq, 2, SSM_GROUPS, HEADS_PER_GROUP)
    return jnp.transpose(t, (0, 3, 1, 2, 4)).reshape(bsz, SSM_GROUPS, seq, 2 * HEADS_PER_GROUP)


def _layer(x, w_in, b_gate, lq1, lk1, lq2, lk2, subln_w, conv_w, conv_b, dtb_f, dtb_b, alog_f, alog_b,
           d_skip, ssm_norm_w, w_branch, w_out, ln1_g, ln1_b, w_router, w_gate_e, w_up_e, w_down_e,
           ln2_g, ln2_b, layer_idx):
    bsz, seq, d = x.shape
    m = bsz * seq
    d_qk = ATTN_HEADS * 2 * ATTN_HEAD_DIM
    d_v = ATTN_HEADS * ATTN_V_DIM
    d_ssm = SSM_HEADS * SSM_HEAD_DIM
    d_conv = d_ssm + 2 * SSM_GROUPS * SSM_STATE
    n_main = 2 * d_qk + d_v + d_ssm + d_conv
    n_dt = 2 * SSM_HEADS
    lambda_init = 0.8 - 0.6 * math.exp(-0.3 * layer_idx)
    cap = CAPACITY_FACTOR * seq // N_EXPERTS
    row = lambda v: v.reshape(1, -1)

    xb = x.reshape(m, d).astype(BF16)
    w_in_t = w_in.T
    w_gate_t = w_in_t[n_main + n_dt:].astype(BF16)

    tn_main = 768
    proj = _matmul_wt(xb, w_in_t, 0, n_main // tn_main, F32, 512, tn_main, "in_proj")
    proj = proj.reshape(bsz, seq, n_main)
    dt_raw = _matmul_wt(xb, w_in_t, n_main // LANES, 1, F32, 512, LANES, "dt_proj")
    dt_raw = dt_raw.reshape(bsz, seq, LANES)

    lane_pad = lambda a, bvec: jnp.pad(jnp.concatenate([a, bvec]), (0, LANES - n_dt)).reshape(1, LANES)
    dt, cs = _dtprep(dt_raw, lane_pad(dtb_f, dtb_b), lane_pad(alog_f, alog_b))
    dtg, csg = _group_heads(dt), _group_heads(cs)
    n_chunks = seq // SSM_CHUNK
    cstg = jnp.transpose(csg.reshape(bsz, SSM_GROUPS, n_chunks, SSM_CHUNK, 2 * HEADS_PER_GROUP),
                         (0, 1, 2, 4, 3))

    y_attn = _attention(proj, row(lq1), row(lk1), row(lq2), row(lk2), row(subln_w), lambda_init, 256)
    y_ssm = _ssd(proj, dtg, csg, cstg, conv_w, row(conv_b), row(jnp.repeat(d_skip, SSM_HEAD_DIM)),
                 row(ssm_norm_w), 2 * d_qk + d_v, 2 * d_qk + d_v + d_ssm)

    u = _merge(xb, w_gate_t, row(b_gate), y_attn.reshape(m, d_v), y_ssm.reshape(m, d_ssm),
               w_branch.astype(BF16), 512, 512)
    w_router_p = jnp.pad(w_router, ((0, 0), (0, LANES - N_EXPERTS))).astype(BF16)
    x1, logits = _outproj_ln(u, w_out.astype(BF16), x.reshape(m, d), row(ln1_g), row(ln1_b),
                             w_router_p, 512, 512)
    x1 = x1.reshape(bsz, seq, d)

    slot, slot_t, gslot = _routing(logits.reshape(bsz, seq, LANES), cap)
    xg = _gather(slot, x1, cap, 1024)
    h = _gateup(xg, w_gate_e, w_up_e, 256)
    gslot_e = jnp.transpose(gslot, (1, 0, 2, 3)).reshape(N_EXPERTS, bsz * cap, 1)
    yg = _down(h, w_down_e, gslot_e, 512)
    return _scatter_ln(slot_t, yg, x1, row(ln2_g), row(ln2_b), cap, 512, 512)


def kernel(x, w_in, b_gate, lambda_q1, lambda_k1, lambda_q2, lambda_k2, attn_subln_w, conv_w, conv_b,
           dt_bias_fwd, dt_bias_bwd, a_log_fwd, a_log_bwd, d_skip, ssm_norm_w, w_branch, w_out,
           ln1_g, ln1_b, w_router, w_gate_e, w_up_e, w_down_e, ln2_g, ln2_b):
    for l in range(w_in.shape[0]):
        x = _layer(x, w_in[l], b_gate[l], lambda_q1[l], lambda_k1[l], lambda_q2[l], lambda_k2[l],
                   attn_subln_w[l], conv_w[l], conv_b[l], dt_bias_fwd[l], dt_bias_bwd[l],
                   a_log_fwd[l], a_log_bwd[l], d_skip[l], ssm_norm_w[l], w_branch[l], w_out[l],
                   ln1_g[l], ln1_b[l], w_router[l], w_gate_e[l], w_up_e[l], w_down_e[l],
                   ln2_g[l], ln2_b[l], l)
    return x
```

```python
import functools
import math

import jax
import jax.numpy as jnp
from jax import lax
from jax.experimental import pallas as pl
from jax.experimental.pallas import tpu as pltpu

F32 = jnp.float32
BF16 = jnp.bfloat16

ATTN_HEADS = 16
ATTN_HEAD_DIM = 64
ATTN_V_DIM = 128
ROPE_THETA = 10000.0
LOG2_E = math.log2(math.e)
SSM_HEAD_DIM = 64
SSM_HEADS = 32
SSM_GROUPS = 8
SSM_STATE = 128
SSM_CHUNK = 128
CONV_WIDTH = 5
N_EXPERTS = 16
CAPACITY_FACTOR = 2
DEPTH = 1
ALPHA = (2.0 * DEPTH) ** 0.25

LANES = 128
SUBLANES = 8
VMEM_LIMIT = 56 * 1024 * 1024
VMEM_LIMIT_BIG_TILES = 62 * 1024 * 1024

HEADS_PER_GROUP = SSM_HEADS // SSM_GROUPS
GROUP_W = HEADS_PER_GROUP * SSM_HEAD_DIM
CONV_PAD_ROWS = SUBLANES


def _cparams(sem, vmem_limit=VMEM_LIMIT):
    return pltpu.CompilerParams(dimension_semantics=sem, vmem_limit_bytes=vmem_limit)


def _sigmoid(x):
    return 1.0 / (1.0 + jnp.exp(-x))


def _silu(x):
    return x * _sigmoid(x)


def _softplus(x):
    return jnp.maximum(x, 0.0) + jnp.log1p(jnp.exp(-jnp.abs(x)))


def _layer_norm(r, g, b):
    mu = jnp.mean(r, axis=-1, keepdims=True)
    d = r - mu
    var = jnp.mean(d * d, axis=-1, keepdims=True)
    return d * lax.rsqrt(var + 1e-5) * g + b


def _mm_wt_kernel(a_ref, w_ref, o_ref, wb_ref):
    @pl.when(pl.program_id(1) == 0)
    def _():
        wb_ref[...] = w_ref[...].astype(BF16)

    o_ref[...] = lax.dot_general(a_ref[...], wb_ref[...], (((1,), (1,)), ((), ())),
                                 preferred_element_type=F32).astype(o_ref.dtype)


def _matmul_wt(a, w_t, blk0, n_blk, out_dtype, tm, tn, name):
    m, k = a.shape
    return pl.pallas_call(
        _mm_wt_kernel,
        grid=(n_blk, m // tm),
        in_specs=[pl.BlockSpec((tm, k), lambda j, i: (i, 0)),
                  pl.BlockSpec((tn, k), lambda j, i: (blk0 + j, 0))],
        out_specs=pl.BlockSpec((tm, tn), lambda j, i: (i, j)),
        out_shape=jax.ShapeDtypeStruct((m, n_blk * tn), out_dtype),
        scratch_shapes=[pltpu.VMEM((tn, k), BF16)],
        compiler_params=_cparams(("arbitrary", "arbitrary"), VMEM_LIMIT_BIG_TILES),
        name=name,
    )(a, w_t)


def _dtprep_kernel(raw_ref, bias_ref, alog_ref, dt_ref, cs_ref):
    seq = raw_ref.shape[1]
    lc = SSM_CHUNK
    dt = _softplus(raw_ref[0] + bias_ref[...])
    dt_ref[0] = dt
    la = dt * (-jnp.exp(alog_ref[...]))
    row = lax.broadcasted_iota(jnp.int32, (lc, lc), 0)
    col = lax.broadcasted_iota(jnp.int32, (lc, lc), 1)
    t_low = jnp.where(row >= col, 1.0, 0.0).astype(F32)
    t_up = jnp.where(row <= col, 1.0, 0.0).astype(F32)
    fwd_lane = lax.broadcasted_iota(jnp.int32, (1, LANES), 1) < SSM_HEADS
    for c in range(seq // lc):
        lac = la[c * lc:(c + 1) * lc]
        f = jnp.dot(t_low, lac, preferred_element_type=F32, precision=lax.Precision.HIGHEST)
        b = jnp.dot(t_up, lac, preferred_element_type=F32, precision=lax.Precision.HIGHEST)
        cs_ref[0, c * lc:(c + 1) * lc, :] = jnp.where(fwd_lane, f, b)


def _dtprep(raw, bias, alog):
    bsz, seq, _ = raw.shape
    blk = pl.BlockSpec((1, seq, LANES), lambda b: (b, 0, 0))
    par = pl.BlockSpec((1, LANES), lambda b: (0, 0))
    return pl.pallas_call(
        _dtprep_kernel,
        grid=(bsz,),
        in_specs=[blk, par, par],
        out_specs=[blk, blk],
        out_shape=[jax.ShapeDtypeStruct(raw.shape, F32)] * 2,
        compiler_params=_cparams(("arbitrary",)),
        name="dt_prep",
    )(raw, bias, alog)


def _attn_kernel(lq1_ref, lk1_ref, lq2_ref, lk2_ref, sw_ref, cq_ref, sq_ref, ck_ref, sk_ref,
                 q_ref, k_ref, v_ref, o_ref, kr_ref, vb_ref, *, lambda_init):
    tq = q_ref.shape[1]
    lane = lax.broadcasted_iota(jnp.int32, (1, LANES), 1)
    first_half = (lane & (ATTN_HEAD_DIM // 2)) == 0
    comp1 = lane < ATTN_HEAD_DIM

    def rope(x, c, s):
        partner = jnp.where(first_half,
                            pltpu.roll(x, LANES - ATTN_HEAD_DIM // 2, 1),
                            pltpu.roll(x, ATTN_HEAD_DIM // 2, 1))
        return x * c + partner * s

    @pl.when(pl.program_id(2) == 0)
    def _():
        kr_ref[...] = rope(k_ref[0], ck_ref[...], sk_ref[...]).astype(BF16)
        vb_ref[...] = v_ref[0].astype(BF16)

    lam = (jnp.exp(jnp.sum(lq1_ref[...] * lk1_ref[...], axis=-1, keepdims=True))
           - jnp.exp(jnp.sum(lq2_ref[...] * lk2_ref[...], axis=-1, keepdims=True)) + lambda_init)

    q = rope(q_ref[0], cq_ref[...], sq_ref[...]) * (ATTN_HEAD_DIM ** -0.5 * LOG2_E)
    qs = jnp.concatenate([jnp.where(comp1, q, 0.0), jnp.where(comp1, 0.0, q)], axis=0).astype(BF16)
    s = lax.dot_general(qs, kr_ref[...], (((1,), (1,)), ((), ())), preferred_element_type=F32)
    p = jnp.exp2(s - jnp.max(s, axis=-1, keepdims=True))
    inv = 1.0 / jnp.sum(p, axis=-1, keepdims=True)
    pc = (p[:tq] * inv[:tq] - p[tq:] * (lam * inv[tq:])).astype(BF16)
    o = jnp.dot(pc, vb_ref[...], preferred_element_type=F32)
    o = o * lax.rsqrt(jnp.mean(o * o, axis=-1, keepdims=True) + 1e-6) * sw_ref[...]
    o_ref[0] = (o * (1.0 - lambda_init)).astype(o_ref.dtype)


def _rope_tables(seq):
    half = ATTN_HEAD_DIM // 2
    inv_freq = ROPE_THETA ** (-jnp.arange(0, ATTN_HEAD_DIM, 2, dtype=F32) / ATTN_HEAD_DIM)
    ang = jnp.arange(seq, dtype=F32)[:, None] * inv_freq[None, :]
    cos, sin = jnp.cos(ang), jnp.sin(ang)
    reps = LANES // half
    sign = jnp.tile(jnp.concatenate([-jnp.ones((half,), F32), jnp.ones((half,), F32)]), reps // 2)
    return jnp.tile(cos, (1, reps)), jnp.tile(sin, (1, reps)) * sign[None, :]


def _attention(proj, lq1, lk1, lq2, lk2, subln_w, lambda_init, tq):
    bsz, seq, _ = proj.shape
    cos, sin = _rope_tables(seq)
    vec = pl.BlockSpec((1, ATTN_HEAD_DIM), lambda b, h, i: (0, 0))
    return pl.pallas_call(
        functools.partial(_attn_kernel, lambda_init=lambda_init),
        grid=(bsz, ATTN_HEADS, seq // tq),
        in_specs=[vec, vec, vec, vec,
                  pl.BlockSpec((1, LANES), lambda b, h, i: (0, 0)),
                  pl.BlockSpec((tq, LANES), lambda b, h, i: (i, 0)),
                  pl.BlockSpec((tq, LANES), lambda b, h, i: (i, 0)),
                  pl.BlockSpec((seq, LANES), lambda b, h, i: (0, 0)),
                  pl.BlockSpec((seq, LANES), lambda b, h, i: (0, 0)),
                  pl.BlockSpec((1, tq, LANES), lambda b, h, i: (b, i, h)),
                  pl.BlockSpec((1, seq, LANES), lambda b, h, i: (b, 0, ATTN_HEADS + h)),
                  pl.BlockSpec((1, seq, LANES), lambda b, h, i: (b, 0, 2 * ATTN_HEADS + h))],
        out_specs=pl.BlockSpec((1, tq, LANES), lambda b, h, i: (b, i, h)),
        out_shape=jax.ShapeDtypeStruct((bsz, seq, ATTN_HEADS * ATTN_V_DIM), BF16),
        scratch_shapes=[pltpu.VMEM((seq, LANES), BF16), pltpu.VMEM((seq, ATTN_V_DIM), BF16)],
        compiler_params=_cparams(("arbitrary", "arbitrary", "arbitrary")),
        name="diff_attention",
    )(lq1, lk1, lq2, lk2, subln_w, cos, sin, cos, sin, proj, proj, proj)


def _ssd_kernel(z_ref, x_ref, b_ref, c_ref, dt_ref, cs_ref, cst_ref,
                cwx_ref, cwb_ref, cwc_ref, cbx_ref, cbb_ref, cbc_ref, dsk_ref, nw_ref,
                o_ref, pad_ref, xs_ref, bm_ref, cm_ref, y_ref, h_ref):
    seq = x_ref.shape[1]
    lc = SSM_CHUNK
    n_chunks = seq // lc
    row_tile = 256

    def conv_silu(in_ref, w_ref, bias_ref, out_ref, width):
        zeros = jnp.zeros((CONV_PAD_ROWS, width), F32)
        pad_ref[0:CONV_PAD_ROWS, 0:width] = zeros
        pad_ref[CONV_PAD_ROWS + seq:2 * CONV_PAD_ROWS + seq, 0:width] = zeros
        pad_ref[CONV_PAD_ROWS:CONV_PAD_ROWS + seq, 0:width] = in_ref[0]
        half = (CONV_WIDTH - 1) // 2
        for t in range(seq // row_tile):
            acc = jnp.broadcast_to(bias_ref[...], (row_tile, width))
            for j in range(CONV_WIDTH):
                start = CONV_PAD_ROWS + t * row_tile + j - half
                acc = acc + pad_ref[start:start + row_tile, 0:width] * w_ref[j:j + 1, :]
            out_ref[t * row_tile:(t + 1) * row_tile, :] = _silu(acc)

    conv_silu(x_ref, cwx_ref, cbx_ref, xs_ref, GROUP_W)
    conv_silu(b_ref, cwb_ref, cbb_ref, bm_ref, SSM_STATE)
    conv_silu(c_ref, cwc_ref, cbc_ref, cm_ref, SSM_STATE)

    head_of_lane = lax.broadcasted_iota(jnp.int32, (1, GROUP_W), 1) // SSM_HEAD_DIM
    row = lax.broadcasted_iota(jnp.int32, (lc, lc), 0)
    col = lax.broadcasted_iota(jnp.int32, (lc, lc), 1)

    def expand(cols, off):
        out = cols[:, off + HEADS_PER_GROUP - 1:off + HEADS_PER_GROUP]
        for r in range(HEADS_PER_GROUP - 2, -1, -1):
            out = jnp.where(head_of_lane == r, cols[:, off + r:off + r + 1], out)
        return out

    def run_direction(reverse):
        off = HEADS_PER_GROUP if reverse else 0
        mask = (row <= col) if reverse else (row >= col)
        edge = 0 if reverse else lc - 1
        h_ref[...] = jnp.zeros_like(h_ref)

        def body(ci, carry):
            c = (n_chunks - 1 - ci) if reverse else ci
            r0 = pl.multiple_of(c * lc, lc)
            xc = xs_ref[pl.ds(r0, lc), :]
            bc = bm_ref[pl.ds(r0, lc), :]
            cc = cm_ref[pl.ds(r0, lc), :].astype(BF16)
            dtc = dt_ref[0, 0, pl.ds(r0, lc), :]
            csc = cs_ref[0, 0, pl.ds(r0, lc), :]
            cst = cst_ref[0, 0, c]
            cs_e = expand(csc, off)
            edge_e = expand(csc[edge:edge + 1, :], off)
            xdt = xc * expand(dtc, off)
            xdt_b = xdt.astype(BF16)
            cb = lax.dot_general(cc, bc.astype(BF16), (((1,), (1,)), ((), ())),
                                 preferred_element_type=F32)
            y = jnp.zeros((lc, GROUP_W), F32)
            for r in range(HEADS_PER_GROUP):
                diff = csc[:, off + r:off + r + 1] - cst[off + r:off + r + 1, :]
                decay = jnp.exp(jnp.where(mask, diff, -jnp.inf))
                yr = jnp.dot((cb * decay).astype(BF16), xdt_b, preferred_element_type=F32)
                y = jnp.where(head_of_lane == r, yr, y)
            h_t = h_ref[...]
            y = y + jnp.dot(cc, h_t.astype(BF16), preferred_element_type=F32) * jnp.exp(cs_e)
            new_state = jnp.dot(bc.T.astype(BF16), (xdt * jnp.exp(edge_e - cs_e)).astype(BF16),
                                preferred_element_type=F32)
            h_ref[...] = h_t * jnp.exp(edge_e) + new_state
            if reverse:
                y_ref[pl.ds(r0, lc), :] += y
            else:
                y_ref[pl.ds(r0, lc), :] = y
            return carry

        lax.fori_loop(0, n_chunks, body, 0, unroll=2)

    run_direction(False)
    run_direction(True)

    for t in range(seq // row_tile):
        rows = slice(t * row_tile, (t + 1) * row_tile)
        y = y_ref[rows, :] + xs_ref[rows, :] * dsk_ref[...]
        y = y * _silu(z_ref[0, rows, :])
        y = y * lax.rsqrt(jnp.mean(y * y, axis=-1, keepdims=True) + 1e-6) * nw_ref[...]
        o_ref[0, rows, :] = y.astype(o_ref.dtype)


def _ssd(proj, dtg, csg, cstg, conv_w, conv_b, dskip_e, norm_w, col0_z, col0_xbc):
    bsz, seq, _ = proj.shape
    d_ssm = SSM_HEADS * SSM_HEAD_DIM
    gn = SSM_GROUPS * SSM_STATE
    n_chunks = seq // SSM_CHUNK
    zb, xb = col0_z // GROUP_W, col0_xbc // GROUP_W
    bb, cb = (col0_xbc + d_ssm) // SSM_STATE, (col0_xbc + d_ssm + gn) // SSM_STATE
    wb, wc = d_ssm // SSM_STATE, (d_ssm + gn) // SSM_STATE
    n_dir_heads = 2 * HEADS_PER_GROUP
    return pl.pallas_call(
        _ssd_kernel,
        grid=(bsz, SSM_GROUPS),
        in_specs=[pl.BlockSpec((1, seq, GROUP_W), lambda b, g: (b, 0, zb + g)),
                  pl.BlockSpec((1, seq, GROUP_W), lambda b, g: (b, 0, xb + g)),
                  pl.BlockSpec((1, seq, SSM_STATE), lambda b, g: (b, 0, bb + g)),
                  pl.BlockSpec((1, seq, SSM_STATE), lambda b, g: (b, 0, cb + g)),
                  pl.BlockSpec((1, 1, seq, n_dir_heads), lambda b, g: (b, g, 0, 0)),
                  pl.BlockSpec((1, 1, seq, n_dir_heads), lambda b, g: (b, g, 0, 0)),
                  pl.BlockSpec((1, 1, n_chunks, n_dir_heads, SSM_CHUNK), lambda b, g: (b, g, 0, 0, 0)),
                  pl.BlockSpec((CONV_WIDTH, GROUP_W), lambda b, g: (0, g)),
                  pl.BlockSpec((CONV_WIDTH, SSM_STATE), lambda b, g: (0, wb + g)),
                  pl.BlockSpec((CONV_WIDTH, SSM_STATE), lambda b, g: (0, wc + g)),
                  pl.BlockSpec((1, GROUP_W), lambda b, g: (0, g)),
                  pl.BlockSpec((1, SSM_STATE), lambda b, g: (0, wb + g)),
                  pl.BlockSpec((1, SSM_STATE), lambda b, g: (0, wc + g)),
                  pl.BlockSpec((1, GROUP_W), lambda b, g: (0, g)),
                  pl.BlockSpec((1, GROUP_W), lambda b, g: (0, g))],
        out_specs=pl.BlockSpec((1, seq, GROUP_W), lambda b, g: (b, 0, g)),
        out_shape=jax.ShapeDtypeStruct((bsz, seq, d_ssm), BF16),
        scratch_shapes=[pltpu.VMEM((seq + 2 * CONV_PAD_ROWS, GROUP_W), F32),
                        pltpu.VMEM((seq, GROUP_W), F32),
                        pltpu.VMEM((seq, SSM_STATE), F32),
                        pltpu.VMEM((seq, SSM_STATE), F32),
                        pltpu.VMEM((seq, GROUP_W), F32),
                        pltpu.VMEM((SSM_STATE, GROUP_W), F32)],
        compiler_params=_cparams(("arbitrary", "arbitrary")),
        name="ssd",
    )(proj, proj, proj, proj, dtg, csg, cstg, conv_w, conv_w, conv_w, conv_b, conv_b, conv_b,
      dskip_e, norm_w)


def _merge_kernel(x_ref, wga_ref, wgs_ref, bga_ref, bgs_ref, ya_ref, ys_ref, wba_ref, wbs_ref, u_ref):
    x = x_ref[...]
    nt = (((1,), (1,)), ((), ()))
    ga = _sigmoid(lax.dot_general(x, wga_ref[...], nt, preferred_element_type=F32) + bga_ref[...])
    gs = _sigmoid(lax.dot_general(x, wgs_ref[...], nt, preferred_element_type=F32) + bgs_ref[...])
    ba = jnp.dot(ya_ref[...], wba_ref[...], preferred_element_type=F32)
    bs = jnp.dot(ys_ref[...], wbs_ref[...], preferred_element_type=F32)
    u_ref[...] = (ga * ba + gs * bs).astype(u_ref.dtype)


def _merge(xb, w_gate, b_gate, y_attn, y_ssm, w_branch, tm, tn):
    m, d = xb.shape
    n = w_branch.shape[1]
    ka, ks = y_attn.shape[1], y_ssm.shape[1]
    nj = n // tn
    return pl.pallas_call(
        _merge_kernel,
        grid=(nj, m // tm),
        in_specs=[pl.BlockSpec((tm, d), lambda j, i: (i, 0)),
                  pl.BlockSpec((tn, d), lambda j, i: (j, 0)),
                  pl.BlockSpec((tn, d), lambda j, i: (nj + j, 0)),
                  pl.BlockSpec((1, tn), lambda j, i: (0, j)),
                  pl.BlockSpec((1, tn), lambda j, i: (0, nj + j)),
                  pl.BlockSpec((tm, ka), lambda j, i: (i, 0)),
                  pl.BlockSpec((tm, ks), lambda j, i: (i, 0)),
                  pl.BlockSpec((ka, tn), lambda j, i: (0, j)),
                  pl.BlockSpec((ks, tn), lambda j, i: (ka // ks, j))],
        out_specs=pl.BlockSpec((tm, tn), lambda j, i: (i, j)),
        out_shape=jax.ShapeDtypeStruct((m, n), BF16),
        compiler_params=_cparams(("arbitrary", "arbitrary")),
        name="gated_merge",
    )(xb, w_gate, w_gate, b_gate, b_gate, y_attn, y_ssm, w_branch, w_branch)


def _outproj_kernel(u_ref, w_ref, x_ref, g_ref, b_ref, wr_ref, x1_ref, lg_ref):
    j = pl.program_id(1)
    tn = w_ref.shape[1]
    n_slabs = x1_ref.shape[1] // tn
    part = jnp.dot(u_ref[...], w_ref[...], preferred_element_type=F32)

    for slab in range(n_slabs):
        @pl.when(j == slab)
        def _(slab=slab):
            x1_ref[:, slab * tn:(slab + 1) * tn] = part

    @pl.when(j == n_slabs - 1)
    def _():
        x1 = _layer_norm(ALPHA * x_ref[...] + x1_ref[...], g_ref[...], b_ref[...])
        x1_ref[...] = x1
        lg_ref[...] = jnp.dot(x1.astype(BF16), wr_ref[...], preferred_element_type=F32)


def _outproj_ln(u, w_out, x, g, b, w_router_p, tm, tn):
    m, kdim = u.shape
    d = w_out.shape[1]
    row = pl.BlockSpec((tm, d), lambda i, j: (i, 0))
    par = pl.BlockSpec((1, d), lambda i, j: (0, 0))
    return pl.pallas_call(
        _outproj_kernel,
        grid=(m // tm, d // tn),
        in_specs=[pl.BlockSpec((tm, kdim), lambda i, j: (i, 0)),
                  pl.BlockSpec((kdim, tn), lambda i, j: (0, j)),
                  pl.BlockSpec((tm, d), lambda i, j: (i, 0), pipeline_mode=pl.Buffered(1)),
                  par, par,
                  pl.BlockSpec((d, LANES), lambda i, j: (0, 0), pipeline_mode=pl.Buffered(1))],
        out_specs=[row, pl.BlockSpec((tm, LANES), lambda i, j: (i, 0))],
        out_shape=[jax.ShapeDtypeStruct((m, d), F32), jax.ShapeDtypeStruct((m, LANES), F32)],
        compiler_params=_cparams(("arbitrary", "arbitrary")),
        name="out_proj_ln1",
    )(u, w_out, x, g, b, w_router_p)


def _routing_kernel(lg_ref, slot_ref, slott_ref, gslot_ref, aff_ref, *, cap):
    seq = lg_ref.shape[1]
    blk = 256
    lg = lg_ref[0]
    valid = lax.broadcasted_iota(jnp.int32, lg.shape, 1) < N_EXPERTS
    lgm = jnp.where(valid, lg, -jnp.inf)
    ex = jnp.exp(lgm - jnp.max(lgm, axis=-1, keepdims=True))
    aff = ex / jnp.sum(ex, axis=-1, keepdims=True)
    aff_t = aff.T[0:N_EXPERTS]
    aff_ref[...] = aff_t
    bits = lax.bitcast_convert_type(aff_t, jnp.int32)

    def count(m):
        return jnp.sum(jnp.where(m, 1.0, 0.0), axis=-1, keepdims=True)

    def search(i, thr):
        cand = thr | jnp.left_shift(jnp.int32(1), 30 - i)
        return jnp.where(count(bits >= cand) >= cap, cand, thr)

    thr = lax.fori_loop(0, 31, search, jnp.zeros((N_EXPERTS, 1), jnp.int32))
    gt = bits > thr
    eq = bits == thr

    r_i = lax.broadcasted_iota(jnp.int32, (blk, blk), 0)
    c_i = lax.broadcasted_iota(jnp.int32, (blk, blk), 1)
    before = jnp.where(r_i < c_i, 1.0, 0.0).astype(BF16)

    def excl_cumsum(m):
        mf = jnp.where(m, 1.0, 0.0)
        carry = jnp.zeros((N_EXPERTS, 1), F32)
        parts = []
        for k in range(seq // blk):
            piece = mf[:, k * blk:(k + 1) * blk]
            parts.append(jnp.dot(piece.astype(BF16), before, preferred_element_type=F32) + carry)
            carry = carry + jnp.sum(piece, axis=-1, keepdims=True)
        return jnp.concatenate(parts, axis=-1)

    need = cap - count(gt)
    sel = gt | (eq & (excl_cumsum(eq) < need))
    slot = jnp.where(sel, excl_cumsum(sel), -1.0)
    slot_ref[0] = slot
    pad = jnp.full((LANES - N_EXPERTS, seq), -1.0, F32)
    slott_ref[0] = jnp.concatenate([slot, pad], axis=0).T

    j_iota = lax.broadcasted_iota(jnp.int32, (cap, seq), 0).astype(F32)

    def gate_of_slot(e, carry):
        hit = slot_ref[0, pl.ds(e, 1), :] == j_iota
        gslot_ref[0, e] = jnp.sum(jnp.where(hit, aff_ref[pl.ds(e, 1), :], 0.0), axis=-1, keepdims=True)
        return carry

    lax.fori_loop(0, N_EXPERTS, gate_of_slot, 0)


def _routing(logits, cap):
    bsz, seq, _ = logits.shape
    return pl.pallas_call(
        functools.partial(_routing_kernel, cap=cap),
        grid=(bsz,),
        in_specs=[pl.BlockSpec((1, seq, LANES), lambda b: (b, 0, 0))],
        out_specs=[pl.BlockSpec((1, N_EXPERTS, seq), lambda b: (b, 0, 0)),
                   pl.BlockSpec((1, seq, LANES), lambda b: (b, 0, 0)),
                   pl.BlockSpec((1, N_EXPERTS, cap, 1), lambda b: (b, 0, 0, 0))],
        out_shape=[jax.ShapeDtypeStruct((bsz, N_EXPERTS, seq), F32),
                   jax.ShapeDtypeStruct((bsz, seq, LANES), F32),
                   jax.ShapeDtypeStruct((bsz, N_EXPERTS, cap, 1), F32)],
        scratch_shapes=[pltpu.VMEM((N_EXPERTS, seq), F32)],
        compiler_params=_cparams(("arbitrary",)),
        name="routing",
    )(logits)


def _gather_kernel(slot_ref, x_ref, o_ref, pick_ref, *, cap):
    seq = x_ref.shape[1]
    td = x_ref.shape[2]

    @pl.when(pl.program_id(1) == 0)
    def _():
        j_iota = lax.broadcasted_iota(jnp.int32, (cap, seq), 0).astype(F32)

        def one_expert(e, carry):
            hit = slot_ref[0, pl.ds(e, 1), :] == j_iota
            pick_ref[pl.ds(pl.multiple_of(e * cap, cap), cap), :] = jnp.where(hit, 1.0, 0.0).astype(BF16)
            return carry

        lax.fori_loop(0, N_EXPERTS, one_expert, 0)

    rows = jnp.dot(pick_ref[...], x_ref[0].astype(BF16), preferred_element_type=F32)
    o_ref[...] = rows.reshape(N_EXPERTS, cap, td).astype(o_ref.dtype)


def _gather(slot, x1, cap, td):
    bsz, seq, d = x1.shape
    return pl.pallas_call(
        functools.partial(_gather_kernel, cap=cap),
        grid=(bsz, d // td),
        in_specs=[pl.BlockSpec((1, N_EXPERTS, seq), lambda b, j: (b, 0, 0)),
                  pl.BlockSpec((1, seq, td), lambda b, j: (b, 0, j))],
        out_specs=pl.BlockSpec((N_EXPERTS, cap, td), lambda b, j: (0, b, j)),
        out_shape=jax.ShapeDtypeStruct((N_EXPERTS, bsz * cap, d), BF16),
        scratch_shapes=[pltpu.VMEM((N_EXPERTS * cap, seq), BF16)],
        compiler_params=_cparams(("arbitrary", "arbitrary")),
        name="moe_gather",
    )(slot, x1)


def _gateup_kernel(xg_ref, wg_ref, wu_ref, h_ref):
    xg = xg_ref[0]
    g = jnp.dot(xg, wg_ref[0].astype(BF16), preferred_element_type=F32)
    u = jnp.dot(xg, wu_ref[0].astype(BF16), preferred_element_type=F32)
    h_ref[0] = (_silu(g) * u).astype(h_ref.dtype)


def _gateup(xg, w_gate_e, w_up_e, tf):
    n_e, rows, d = xg.shape
    ff = w_gate_e.shape[2]
    wspec = pl.BlockSpec((1, d, tf), lambda e, f: (e, 0, f))
    return pl.pallas_call(
        _gateup_kernel,
        grid=(n_e, ff // tf),
        in_specs=[pl.BlockSpec((1, rows, d), lambda e, f: (e, 0, 0), pipeline_mode=pl.Buffered(1)),
                  wspec, wspec],
        out_specs=pl.BlockSpec((1, rows, tf), lambda e, f: (e, 0, f)),
        out_shape=jax.ShapeDtypeStruct((n_e, rows, ff), BF16),
        compiler_params=_cparams(("arbitrary", "arbitrary")),
        name="moe_gate_up",
    )(xg, w_gate_e, w_up_e)


def _down_kernel(h_ref, wd_ref, gs_ref, y_ref):
    y = jnp.dot(h_ref[0], wd_ref[0].astype(BF16), preferred_element_type=F32)
    y_ref[0] = (y * gs_ref[0]).astype(y_ref.dtype)


def _down(h, w_down_e, gslot, td):
    n_e, rows, ff = h.shape
    d = w_down_e.shape[2]
    return pl.pallas_call(
        _down_kernel,
        grid=(n_e, d // td),
        in_specs=[pl.BlockSpec((1, rows, ff), lambda e, j: (e, 0, 0)),
                  pl.BlockSpec((1, ff, td), lambda e, j: (e, 0, j)),
                  pl.BlockSpec((1, rows, 1), lambda e, j: (e, 0, 0))],
        out_specs=pl.BlockSpec((1, rows, td), lambda e, j: (e, 0, j)),
        out_shape=jax.ShapeDtypeStruct((n_e, rows, d), BF16),
        compiler_params=_cparams(("arbitrary", "arbitrary")),
        name="moe_down",
    )(h, w_down_e, gslot)


def _scatter_kernel(slott_ref, yg_ref, x1_ref, g_ref, b_ref, o_ref, put_ref, *, cap):
    dj = pl.program_id(2)
    ts = o_ref.shape[1]
    td = yg_ref.shape[2]
    n_slabs = o_ref.shape[2] // td

    @pl.when(dj == 0)
    def _():
        st = slott_ref[0]
        lane = lax.broadcasted_iota(jnp.int32, st.shape, 1)
        j_iota = lax.broadcasted_iota(jnp.int32, (ts, cap), 1).astype(F32)
        for e in range(N_EXPERTS):
            col = jnp.sum(jnp.where(lane == e, st, 0.0), axis=-1, keepdims=True)
            put_ref[:, e * cap:(e + 1) * cap] = jnp.where(col == j_iota, 1.0, 0.0).astype(BF16)

    part = jnp.dot(put_ref[...], yg_ref[...].reshape(N_EXPERTS * cap, td), preferred_element_type=F32)

    for slab in range(n_slabs):
        @pl.when(dj == slab)
        def _(slab=slab):
            o_ref[0, :, slab * td:(slab + 1) * td] = part

    @pl.when(dj == n_slabs - 1)
    def _():
        o_ref[0] = _layer_norm(ALPHA * x1_ref[0] + o_ref[0], g_ref[...], b_ref[...])


def _scatter_ln(slot_t, yg, x1, g, b, cap, ts, td):
    bsz, seq, d = x1.shape
    par = pl.BlockSpec((1, d), lambda bi, i, j: (0, 0))
    return pl.pallas_call(
        functools.partial(_scatter_kernel, cap=cap),
        grid=(bsz, seq // ts, d // td),
        in_specs=[pl.BlockSpec((1, ts, LANES), lambda bi, i, j: (bi, i, 0)),
                  pl.BlockSpec((N_EXPERTS, cap, td), lambda bi, i, j: (0, bi, j)),
                  pl.BlockSpec((1, ts, d), lambda bi, i, j: (bi, i, 0), pipeline_mode=pl.Buffered(1)),
                  par, par],
        out_specs=pl.BlockSpec((1, ts, d), lambda bi, i, j: (bi, i, 0)),
        out_shape=jax.ShapeDtypeStruct((bsz, seq, d), F32),
        scratch_shapes=[pltpu.VMEM((ts, N_EXPERTS * cap), BF16)],
        compiler_params=_cparams(("arbitrary", "arbitrary", "arbitrary")),
        name="moe_scatter_ln2",
    )(slot_t, yg, x1, g, b)


def _group_heads(t):
    bsz, seq, _ = t.shape
    t = t[:, :, :2 * SSM_HEADS].reshape(bsz, seq, 2, SSM_GROUPS, HEADS_PER_GROUP)
    return jnp.transpose(t, (0, 3, 1, 2, 4)).reshape(bsz, SSM_GROUPS, seq, 2 * HEADS_PER_GROUP)


def _layer(x, w_in, b_gate, lq1, lk1, lq2, lk2, subln_w, conv_w, conv_b, dtb_f, dtb_b, alog_f, alog_b,
           d_skip, ssm_norm_w, w_branch, w_out, ln1_g, ln1_b, w_router, w_gate_e, w_up_e, w_down_e,
           ln2_g, ln2_b, layer_idx):
    bsz, seq, d = x.shape
    m = bsz * seq
    d_qk = ATTN_HEADS * 2 * ATTN_HEAD_DIM
    d_v = ATTN_HEADS * ATTN_V_DIM
    d_ssm = SSM_HEADS * SSM_HEAD_DIM
    d_conv = d_ssm + 2 * SSM_GROUPS * SSM_STATE
    n_main = 2 * d_qk + d_v + d_ssm + d_conv
    n_dt = 2 * SSM_HEADS
    lambda_init = 0.8 - 0.6 * math.exp(-0.3 * layer_idx)
    cap = CAPACITY_FACTOR * seq // N_EXPERTS
    row = lambda v: v.reshape(1, -1)

    xb = x.reshape(m, d).astype(BF16)
    w_in_t = w_in.T
    w_gate_t = w_in_t[n_main + n_dt:].astype(BF16)

    tn_main = 768
    proj = _matmul_wt(xb, w_in_t, 0, n_main // tn_main, F32, 1024, tn_main, "in_proj")
    proj = proj.reshape(bsz, seq, n_main)
    dt_raw = _matmul_wt(xb, w_in_t, n_main // LANES, 1, F32, 512, LANES, "dt_proj")
    dt_raw = dt_raw.reshape(bsz, seq, LANES)

    lane_pad = lambda a, bvec: jnp.pad(jnp.concatenate([a, bvec]), (0, LANES - n_dt)).reshape(1, LANES)
    dt, cs = _dtprep(dt_raw, lane_pad(dtb_f, dtb_b), lane_pad(alog_f, alog_b))
    dtg, csg = _group_heads(dt), _group_heads(cs)
    n_chunks = seq // SSM_CHUNK
    cstg = jnp.transpose(csg.reshape(bsz, SSM_GROUPS, n_chunks, SSM_CHUNK, 2 * HEADS_PER_GROUP),
                         (0, 1, 2, 4, 3))

    y_attn = _attention(proj, row(lq1), row(lk1), row(lq2), row(lk2), row(subln_w), lambda_init, 256)
    y_ssm = _ssd(proj, dtg, csg, cstg, conv_w, row(conv_b), row(jnp.repeat(d_skip, SSM_HEAD_DIM)),
                 row(ssm_norm_w), 2 * d_qk + d_v, 2 * d_qk + d_v + d_ssm)

    u = _merge(xb, w_gate_t, row(b_gate), y_attn.reshape(m, d_v), y_ssm.reshape(m, d_ssm),
               w_branch.astype(BF16), 512, 512)
    w_router_p = jnp.pad(w_router, ((0, 0), (0, LANES - N_EXPERTS))).astype(BF16)
    x1, logits = _outproj_ln(u, w_out.astype(BF16), x.reshape(m, d), row(ln1_g), row(ln1_b),
                             w_router_p, 512, 512)
    x1 = x1.reshape(bsz, seq, d)

    slot, slot_t, gslot = _routing(logits.reshape(bsz, seq, LANES), cap)
    xg = _gather(slot, x1, cap, 512)
    h = _gateup(xg, w_gate_e, w_up_e, 512)
    gslot_e = jnp.transpose(gslot, (1, 0, 2, 3)).reshape(N_EXPERTS, bsz * cap, 1)
    yg = _down(h, w_down_e, gslot_e, 1024)
    return _scatter_ln(slot_t, yg, x1, row(ln2_g), row(ln2_b), cap, 512, 512)


def kernel(x, w_in, b_gate, lambda_q1, lambda_k1, lambda_q2, lambda_k2, attn_subln_w, conv_w, conv_b,
           dt_bias_fwd, dt_bias_bwd, a_log_fwd, a_log_bwd, d_skip, ssm_norm_w, w_branch, w_out,
           ln1_g, ln1_b, w_router, w_gate_e, w_up_e, w_down_e, ln2_g, ln2_b):
    for l in range(w_in.shape[0]):
        x = _layer(x, w_in[l], b_gate[l], lambda_q1[l], lambda_k1[l], lambda_q2[l], lambda_k2[l],
                   attn_subln_w[l], conv_w[l], conv_b[l], dt_bias_fwd[l], dt_bias_bwd[l],
                   a_log_fwd[l], a_log_bwd[l], d_skip[l], ssm_norm_w[l], w_branch[l], w_out[l],
                   ln1_g[l], ln1_b[l], w_router[l], w_gate_e[l], w_up_e[l], w_down_e[l],
                   ln2_g[l], ln2_b[l], l)
    return x
```

```python
import functools
import math

import jax
import jax.numpy as jnp
from jax import lax
from jax.experimental import pallas as pl
from jax.experimental.pallas import tpu as pltpu

F32 = jnp.float32
BF16 = jnp.bfloat16

ATTN_HEADS = 16
ATTN_HEAD_DIM = 64
ATTN_V_DIM = 128
ROPE_THETA = 10000.0
LOG2_E = math.log2(math.e)
SSM_HEAD_DIM = 64
SSM_HEADS = 32
SSM_GROUPS = 8
SSM_STATE = 128
SSM_CHUNK = 128
CONV_WIDTH = 5
N_EXPERTS = 16
CAPACITY_FACTOR = 2
DEPTH = 1
ALPHA = (2.0 * DEPTH) ** 0.25

LANES = 128
SUBLANES = 8
VMEM_LIMIT = 56 * 1024 * 1024
VMEM_LIMIT_BIG_TILES = 62 * 1024 * 1024

HEADS_PER_GROUP = SSM_HEADS // SSM_GROUPS
GROUP_W = HEADS_PER_GROUP * SSM_HEAD_DIM
CONV_PAD_ROWS = SUBLANES


def _cparams(sem, vmem_limit=VMEM_LIMIT):
    return pltpu.CompilerParams(dimension_semantics=sem, vmem_limit_bytes=vmem_limit)


def _sigmoid(x):
    return 1.0 / (1.0 + jnp.exp(-x))


def _silu(x):
    return x * _sigmoid(x)


def _softplus(x):
    return jnp.maximum(x, 0.0) + jnp.log1p(jnp.exp(-jnp.abs(x)))


def _layer_norm(r, g, b):
    mu = jnp.mean(r, axis=-1, keepdims=True)
    d = r - mu
    var = jnp.mean(d * d, axis=-1, keepdims=True)
    return d * lax.rsqrt(var + 1e-5) * g + b


def _mm_wt_kernel(a_ref, w_ref, o_ref, wb_ref):
    @pl.when(pl.program_id(1) == 0)
    def _():
        wb_ref[...] = w_ref[...].astype(BF16)

    o_ref[...] = lax.dot_general(a_ref[...], wb_ref[...], (((1,), (1,)), ((), ())),
                                 preferred_element_type=F32).astype(o_ref.dtype)


def _matmul_wt(a, w_t, blk0, n_blk, out_dtype, tm, tn, name):
    m, k = a.shape
    return pl.pallas_call(
        _mm_wt_kernel,
        grid=(n_blk, m // tm),
        in_specs=[pl.BlockSpec((tm, k), lambda j, i: (i, 0)),
                  pl.BlockSpec((tn, k), lambda j, i: (blk0 + j, 0))],
        out_specs=pl.BlockSpec((tm, tn), lambda j, i: (i, j)),
        out_shape=jax.ShapeDtypeStruct((m, n_blk * tn), out_dtype),
        scratch_shapes=[pltpu.VMEM((tn, k), BF16)],
        compiler_params=_cparams(("arbitrary", "arbitrary"), VMEM_LIMIT_BIG_TILES),
        name=name,
    )(a, w_t)


def _dtprep_kernel(raw_ref, bias_ref, alog_ref, dt_ref, cs_ref):
    seq = raw_ref.shape[1]
    lc = SSM_CHUNK
    dt = _softplus(raw_ref[0] + bias_ref[...])
    dt_ref[0] = dt
    la = dt * (-jnp.exp(alog_ref[...]))
    row = lax.broadcasted_iota(jnp.int32, (lc, lc), 0)
    col = lax.broadcasted_iota(jnp.int32, (lc, lc), 1)
    t_low = jnp.where(row >= col, 1.0, 0.0).astype(F32)
    t_up = jnp.where(row <= col, 1.0, 0.0).astype(F32)
    fwd_lane = lax.broadcasted_iota(jnp.int32, (1, LANES), 1) < SSM_HEADS
    for c in range(seq // lc):
        lac = la[c * lc:(c + 1) * lc]
        f = jnp.dot(t_low, lac, preferred_element_type=F32, precision=lax.Precision.HIGHEST)
        b = jnp.dot(t_up, lac, preferred_element_type=F32, precision=lax.Precision.HIGHEST)
        cs_ref[0, c * lc:(c + 1) * lc, :] = jnp.where(fwd_lane, f, b)


def _dtprep(raw, bias, alog):
    bsz, seq, _ = raw.shape
    blk = pl.BlockSpec((1, seq, LANES), lambda b: (b, 0, 0))
    par = pl.BlockSpec((1, LANES), lambda b: (0, 0))
    return pl.pallas_call(
        _dtprep_kernel,
        grid=(bsz,),
        in_specs=[blk, par, par],
        out_specs=[blk, blk],
        out_shape=[jax.ShapeDtypeStruct(raw.shape, F32)] * 2,
        compiler_params=_cparams(("arbitrary",)),
        name="dt_prep",
    )(raw, bias, alog)


def _attn_kernel(lq1_ref, lk1_ref, lq2_ref, lk2_ref, sw_ref, cos_ref, sin_ref,
                 q_ref, k_ref, v_ref, o_ref, kr_ref, vb_ref, qs_ref, sa_ref, sb_ref, *, lambda_init, tq):
    seq = q_ref.shape[1]
    n_tiles = seq // tq
    lane = lax.broadcasted_iota(jnp.int32, (1, LANES), 1)
    first_half = (lane & (ATTN_HEAD_DIM // 2)) == 0
    comp1 = lane < ATTN_HEAD_DIM

    def rope(x, c, s):
        partner = jnp.where(first_half,
                            pltpu.roll(x, LANES - ATTN_HEAD_DIM // 2, 1),
                            pltpu.roll(x, ATTN_HEAD_DIM // 2, 1))
        return x * c + partner * s

    kr_ref[...] = rope(k_ref[0], cos_ref[...], sin_ref[...]).astype(BF16)
    vb_ref[...] = v_ref[0].astype(BF16)
    q = rope(q_ref[0], cos_ref[...], sin_ref[...]) * (ATTN_HEAD_DIM ** -0.5 * LOG2_E)
    for t in range(n_tiles):
        qt = q[t * tq:(t + 1) * tq]
        qs_ref[t, 0:tq, :] = jnp.where(comp1, qt, 0.0).astype(BF16)
        qs_ref[t, tq:2 * tq, :] = jnp.where(comp1, 0.0, qt).astype(BF16)

    lam = (jnp.exp(jnp.sum(lq1_ref[...] * lk1_ref[...], axis=-1, keepdims=True))
           - jnp.exp(jnp.sum(lq2_ref[...] * lk2_ref[...], axis=-1, keepdims=True)) + lambda_init)

    def scores(t, dst_ref):
        dst_ref[...] = lax.dot_general(qs_ref[t], kr_ref[...], (((1,), (1,)), ((), ())),
                                       preferred_element_type=F32)

    def finish(t, src_ref):
        s = src_ref[...]
        p = jnp.exp2(s - jnp.max(s, axis=-1, keepdims=True))
        inv = 1.0 / jnp.sum(p, axis=-1, keepdims=True)
        pc = (p[:tq] * inv[:tq] - p[tq:] * (lam * inv[tq:])).astype(BF16)
        o = jnp.dot(pc, vb_ref[...], preferred_element_type=F32)
        o = o * lax.rsqrt(jnp.mean(o * o, axis=-1, keepdims=True) + 1e-6) * sw_ref[...]
        o_ref[0, pl.ds(pl.multiple_of(t * tq, tq), tq), :] = (o * (1.0 - lambda_init)).astype(o_ref.dtype)

    scores(0, sa_ref)

    def pair(j, carry):
        t = 2 * j
        scores(t + 1, sb_ref)
        finish(t, sa_ref)
        scores(t + 2, sa_ref)
        finish(t + 1, sb_ref)
        return carry

    lax.fori_loop(0, n_tiles // 2 - 1, pair, 0)
    scores(n_tiles - 1, sb_ref)
    finish(n_tiles - 2, sa_ref)
    finish(n_tiles - 1, sb_ref)


def _rope_tables(seq):
    half = ATTN_HEAD_DIM // 2
    inv_freq = ROPE_THETA ** (-jnp.arange(0, ATTN_HEAD_DIM, 2, dtype=F32) / ATTN_HEAD_DIM)
    ang = jnp.arange(seq, dtype=F32)[:, None] * inv_freq[None, :]
    cos, sin = jnp.cos(ang), jnp.sin(ang)
    reps = LANES // half
    sign = jnp.tile(jnp.concatenate([-jnp.ones((half,), F32), jnp.ones((half,), F32)]), reps // 2)
    return jnp.tile(cos, (1, reps)), jnp.tile(sin, (1, reps)) * sign[None, :]


def _attention(proj, lq1, lk1, lq2, lk2, subln_w, lambda_init, tq):
    bsz, seq, _ = proj.shape
    cos, sin = _rope_tables(seq)
    assert seq % (2 * tq) == 0 and seq // tq >= 2
    vec = pl.BlockSpec((1, ATTN_HEAD_DIM), lambda b, h: (0, 0))
    table = pl.BlockSpec((seq, LANES), lambda b, h: (0, 0))
    return pl.pallas_call(
        functools.partial(_attn_kernel, lambda_init=lambda_init, tq=tq),
        grid=(bsz, ATTN_HEADS),
        in_specs=[vec, vec, vec, vec,
                  pl.BlockSpec((1, LANES), lambda b, h: (0, 0)),
                  table, table,
                  pl.BlockSpec((1, seq, LANES), lambda b, h: (b, 0, h)),
                  pl.BlockSpec((1, seq, LANES), lambda b, h: (b, 0, ATTN_HEADS + h)),
                  pl.BlockSpec((1, seq, LANES), lambda b, h: (b, 0, 2 * ATTN_HEADS + h))],
        out_specs=pl.BlockSpec((1, seq, LANES), lambda b, h: (b, 0, h)),
        out_shape=jax.ShapeDtypeStruct((bsz, seq, ATTN_HEADS * ATTN_V_DIM), BF16),
        scratch_shapes=[pltpu.VMEM((seq, LANES), BF16), pltpu.VMEM((seq, ATTN_V_DIM), BF16),
                        pltpu.VMEM((seq // tq, 2 * tq, LANES), BF16),
                        pltpu.VMEM((2 * tq, seq), F32), pltpu.VMEM((2 * tq, seq), F32)],
        compiler_params=_cparams(("arbitrary", "arbitrary")),
        name="diff_attention",
    )(lq1, lk1, lq2, lk2, subln_w, cos, sin, proj, proj, proj)


def _ssd_kernel(z_ref, x_ref, b_ref, c_ref, dt_ref, cs_ref, cst_ref,
                cwx_ref, cwb_ref, cwc_ref, cbx_ref, cbb_ref, cbc_ref, dsk_ref, nw_ref,
                o_ref, pad_ref, xs_ref, bm_ref, cm_ref, y_ref, h_ref):
    seq = x_ref.shape[1]
    lc = SSM_CHUNK
    n_chunks = seq // lc
    row_tile = 256

    def conv_silu(in_ref, w_ref, bias_ref, out_ref, width):
        zeros = jnp.zeros((CONV_PAD_ROWS, width), F32)
        pad_ref[0:CONV_PAD_ROWS, 0:width] = zeros
        pad_ref[CONV_PAD_ROWS + seq:2 * CONV_PAD_ROWS + seq, 0:width] = zeros
        pad_ref[CONV_PAD_ROWS:CONV_PAD_ROWS + seq, 0:width] = in_ref[0]
        half = (CONV_WIDTH - 1) // 2
        for t in range(seq // row_tile):
            acc = jnp.broadcast_to(bias_ref[...], (row_tile, width))
            for j in range(CONV_WIDTH):
                start = CONV_PAD_ROWS + t * row_tile + j - half
                acc = acc + pad_ref[start:start + row_tile, 0:width] * w_ref[j:j + 1, :]
            out_ref[t * row_tile:(t + 1) * row_tile, :] = _silu(acc)

    conv_silu(x_ref, cwx_ref, cbx_ref, xs_ref, GROUP_W)
    conv_silu(b_ref, cwb_ref, cbb_ref, bm_ref, SSM_STATE)
    conv_silu(c_ref, cwc_ref, cbc_ref, cm_ref, SSM_STATE)

    head_of_lane = lax.broadcasted_iota(jnp.int32, (1, GROUP_W), 1) // SSM_HEAD_DIM
    row = lax.broadcasted_iota(jnp.int32, (lc, lc), 0)
    col = lax.broadcasted_iota(jnp.int32, (lc, lc), 1)

    def expand(cols, off):
        out = cols[:, off + HEADS_PER_GROUP - 1:off + HEADS_PER_GROUP]
        for r in range(HEADS_PER_GROUP - 2, -1, -1):
            out = jnp.where(head_of_lane == r, cols[:, off + r:off + r + 1], out)
        return out

    def run_direction(reverse):
        off = HEADS_PER_GROUP if reverse else 0
        mask = (row <= col) if reverse else (row >= col)
        edge = 0 if reverse else lc - 1
        h_ref[...] = jnp.zeros_like(h_ref)

        def body(ci, carry):
            c = (n_chunks - 1 - ci) if reverse else ci
            r0 = pl.multiple_of(c * lc, lc)
            xc = xs_ref[pl.ds(r0, lc), :]
            bc = bm_ref[pl.ds(r0, lc), :]
            cc = cm_ref[pl.ds(r0, lc), :].astype(BF16)
            dtc = dt_ref[0, 0, pl.ds(r0, lc), :]
            csc = cs_ref[0, 0, pl.ds(r0, lc), :]
            cst = cst_ref[0, 0, c]
            cs_e = expand(csc, off)
            edge_e = expand(csc[edge:edge + 1, :], off)
            xdt = xc * expand(dtc, off)
            xdt_b = xdt.astype(BF16)
            cb = lax.dot_general(cc, bc.astype(BF16), (((1,), (1,)), ((), ())),
                                 preferred_element_type=F32)
            y = jnp.zeros((lc, GROUP_W), F32)
            for r in range(HEADS_PER_GROUP):
                diff = csc[:, off + r:off + r + 1] - cst[off + r:off + r + 1, :]
                decay = jnp.exp(jnp.where(mask, diff, -jnp.inf))
                yr = jnp.dot((cb * decay).astype(BF16), xdt_b, preferred_element_type=F32)
                y = jnp.where(head_of_lane == r, yr, y)
            h_t = h_ref[...]
            y = y + jnp.dot(cc, h_t.astype(BF16), preferred_element_type=F32) * jnp.exp(cs_e)
            new_state = jnp.dot(bc.T.astype(BF16), (xdt * jnp.exp(edge_e - cs_e)).astype(BF16),
                                preferred_element_type=F32)
            h_ref[...] = h_t * jnp.exp(edge_e) + new_state
            if reverse:
                y_ref[pl.ds(r0, lc), :] += y
            else:
                y_ref[pl.ds(r0, lc), :] = y
            return carry

        lax.fori_loop(0, n_chunks, body, 0, unroll=4)

    run_direction(False)
    run_direction(True)

    for t in range(seq // row_tile):
        rows = slice(t * row_tile, (t + 1) * row_tile)
        y = y_ref[rows, :] + xs_ref[rows, :] * dsk_ref[...]
        y = y * _silu(z_ref[0, rows, :])
        y = y * lax.rsqrt(jnp.mean(y * y, axis=-1, keepdims=True) + 1e-6) * nw_ref[...]
        o_ref[0, rows, :] = y.astype(o_ref.dtype)


def _ssd(proj, dtg, csg, cstg, conv_w, conv_b, dskip_e, norm_w, col0_z, col0_xbc):
    bsz, seq, _ = proj.shape
    d_ssm = SSM_HEADS * SSM_HEAD_DIM
    gn = SSM_GROUPS * SSM_STATE
    n_chunks = seq // SSM_CHUNK
    zb, xb = col0_z // GROUP_W, col0_xbc // GROUP_W
    bb, cb = (col0_xbc + d_ssm) // SSM_STATE, (col0_xbc + d_ssm + gn) // SSM_STATE
    wb, wc = d_ssm // SSM_STATE, (d_ssm + gn) // SSM_STATE
    n_dir_heads = 2 * HEADS_PER_GROUP
    return pl.pallas_call(
        _ssd_kernel,
        grid=(bsz, SSM_GROUPS),
        in_specs=[pl.BlockSpec((1, seq, GROUP_W), lambda b, g: (b, 0, zb + g)),
                  pl.BlockSpec((1, seq, GROUP_W), lambda b, g: (b, 0, xb + g)),
                  pl.BlockSpec((1, seq, SSM_STATE), lambda b, g: (b, 0, bb + g)),
                  pl.BlockSpec((1, seq, SSM_STATE), lambda b, g: (b, 0, cb + g)),
                  pl.BlockSpec((1, 1, seq, n_dir_heads), lambda b, g: (b, g, 0, 0)),
                  pl.BlockSpec((1, 1, seq, n_dir_heads), lambda b, g: (b, g, 0, 0)),
                  pl.BlockSpec((1, 1, n_chunks, n_dir_heads, SSM_CHUNK), lambda b, g: (b, g, 0, 0, 0)),
                  pl.BlockSpec((CONV_WIDTH, GROUP_W), lambda b, g: (0, g)),
                  pl.BlockSpec((CONV_WIDTH, SSM_STATE), lambda b, g: (0, wb + g)),
                  pl.BlockSpec((CONV_WIDTH, SSM_STATE), lambda b, g: (0, wc + g)),
                  pl.BlockSpec((1, GROUP_W), lambda b, g: (0, g)),
                  pl.BlockSpec((1, SSM_STATE), lambda b, g: (0, wb + g)),
                  pl.BlockSpec((1, SSM_STATE), lambda b, g: (0, wc + g)),
                  pl.BlockSpec((1, GROUP_W), lambda b, g: (0, g)),
                  pl.BlockSpec((1, GROUP_W), lambda b, g: (0, g))],
        out_specs=pl.BlockSpec((1, seq, GROUP_W), lambda b, g: (b, 0, g)),
        out_shape=jax.ShapeDtypeStruct((bsz, seq, d_ssm), BF16),
        scratch_shapes=[pltpu.VMEM((seq + 2 * CONV_PAD_ROWS, GROUP_W), F32),
                        pltpu.VMEM((seq, GROUP_W), F32),
                        pltpu.VMEM((seq, SSM_STATE), F32),
                        pltpu.VMEM((seq, SSM_STATE), F32),
                        pltpu.VMEM((seq, GROUP_W), F32),
                        pltpu.VMEM((SSM_STATE, GROUP_W), F32)],
        compiler_params=_cparams(("arbitrary", "arbitrary")),
        name="ssd",
    )(proj, proj, proj, proj, dtg, csg, cstg, conv_w, conv_w, conv_w, conv_b, conv_b, conv_b,
      dskip_e, norm_w)


def _merge_kernel(x_ref, wga_ref, wgs_ref, bga_ref, bgs_ref, ya_ref, ys_ref, wba_ref, wbs_ref, u_ref,
                  wba_b_ref, wbs_b_ref):
    @pl.when(pl.program_id(1) == 0)
    def _():
        wba_b_ref[...] = wba_ref[...].astype(BF16)
        wbs_b_ref[...] = wbs_ref[...].astype(BF16)

    x = x_ref[...]
    nt = (((1,), (1,)), ((), ()))
    ga = _sigmoid(lax.dot_general(x, wga_ref[...], nt, preferred_element_type=F32) + bga_ref[...])
    gs = _sigmoid(lax.dot_general(x, wgs_ref[...], nt, preferred_element_type=F32) + bgs_ref[...])
    ba = jnp.dot(ya_ref[...], wba_b_ref[...], preferred_element_type=F32)
    bs = jnp.dot(ys_ref[...], wbs_b_ref[...], preferred_element_type=F32)
    u_ref[...] = (ga * ba + gs * bs).astype(u_ref.dtype)


def _merge(xb, w_gate, b_gate, y_attn, y_ssm, w_branch, tm, tn):
    m, d = xb.shape
    n = w_branch.shape[1]
    ka, ks = y_attn.shape[1], y_ssm.shape[1]
    nj = n // tn
    return pl.pallas_call(
        _merge_kernel,
        grid=(nj, m // tm),
        in_specs=[pl.BlockSpec((tm, d), lambda j, i: (i, 0)),
                  pl.BlockSpec((tn, d), lambda j, i: (j, 0)),
                  pl.BlockSpec((tn, d), lambda j, i: (nj + j, 0)),
                  pl.BlockSpec((1, tn), lambda j, i: (0, j)),
                  pl.BlockSpec((1, tn), lambda j, i: (0, nj + j)),
                  pl.BlockSpec((tm, ka), lambda j, i: (i, 0)),
                  pl.BlockSpec((tm, ks), lambda j, i: (i, 0)),
                  pl.BlockSpec((ka, tn), lambda j, i: (0, j)),
                  pl.BlockSpec((ks, tn), lambda j, i: (ka // ks, j))],
        out_specs=pl.BlockSpec((tm, tn), lambda j, i: (i, j)),
        out_shape=jax.ShapeDtypeStruct((m, n), BF16),
        scratch_shapes=[pltpu.VMEM((ka, tn), BF16), pltpu.VMEM((ks, tn), BF16)],
        compiler_params=_cparams(("arbitrary", "arbitrary"), VMEM_LIMIT_BIG_TILES),
        name="gated_merge",
    )(xb, w_gate, w_gate, b_gate, b_gate, y_attn, y_ssm, w_branch, w_branch)


def _outproj_kernel(u_ref, w_ref, x_ref, g_ref, b_ref, wr_ref, x1_ref, lg_ref):
    j = pl.program_id(1)
    tn = w_ref.shape[1]
    n_slabs = x1_ref.shape[1] // tn
    part = jnp.dot(u_ref[...], w_ref[...], preferred_element_type=F32)

    for slab in range(n_slabs):
        @pl.when(j == slab)
        def _(slab=slab):
            x1_ref[:, slab * tn:(slab + 1) * tn] = part

    @pl.when(j == n_slabs - 1)
    def _():
        x1 = _layer_norm(ALPHA * x_ref[...] + x1_ref[...], g_ref[...], b_ref[...])
        x1_ref[...] = x1
        lg_ref[...] = jnp.dot(x1.astype(BF16), wr_ref[...], preferred_element_type=F32)


def _outproj_ln(u, w_out, x, g, b, w_router_p, tm, tn):
    m, kdim = u.shape
    d = w_out.shape[1]
    row = pl.BlockSpec((tm, d), lambda i, j: (i, 0))
    par = pl.BlockSpec((1, d), lambda i, j: (0, 0))
    return pl.pallas_call(
        _outproj_kernel,
        grid=(m // tm, d // tn),
        in_specs=[pl.BlockSpec((tm, kdim), lambda i, j: (i, 0)),
                  pl.BlockSpec((kdim, tn), lambda i, j: (0, j)),
                  pl.BlockSpec((tm, d), lambda i, j: (i, 0)),
                  par, par,
                  pl.BlockSpec((d, LANES), lambda i, j: (0, 0), pipeline_mode=pl.Buffered(1))],
        out_specs=[row, pl.BlockSpec((tm, LANES), lambda i, j: (i, 0))],
        out_shape=[jax.ShapeDtypeStruct((m, d), F32), jax.ShapeDtypeStruct((m, LANES), F32)],
        compiler_params=_cparams(("arbitrary", "arbitrary"), VMEM_LIMIT_BIG_TILES),
        name="out_proj_ln1",
    )(u, w_out, x, g, b, w_router_p)


def _routing_kernel(lg_ref, slot_ref, slott_ref, gslot_ref, aff_ref, *, cap):
    seq = lg_ref.shape[1]
    blk = 256
    lg = lg_ref[0]
    valid = lax.broadcasted_iota(jnp.int32, lg.shape, 1) < N_EXPERTS
    lgm = jnp.where(valid, lg, -jnp.inf)
    ex = jnp.exp(lgm - jnp.max(lgm, axis=-1, keepdims=True))
    aff = ex / jnp.sum(ex, axis=-1, keepdims=True)
    aff_t = aff.T[0:N_EXPERTS]
    aff_ref[...] = aff_t
    bits = lax.bitcast_convert_type(aff_t, jnp.int32)

    def count(m):
        return jnp.sum(jnp.where(m, 1.0, 0.0), axis=-1, keepdims=True)

    def search(i, thr):
        cand = thr | jnp.left_shift(jnp.int32(1), 30 - i)
        return jnp.where(count(bits >= cand) >= cap, cand, thr)

    thr = lax.fori_loop(0, 31, search, jnp.zeros((N_EXPERTS, 1), jnp.int32))
    gt = bits > thr
    eq = bits == thr

    r_i = lax.broadcasted_iota(jnp.int32, (blk, blk), 0)
    c_i = lax.broadcasted_iota(jnp.int32, (blk, blk), 1)
    before = jnp.where(r_i < c_i, 1.0, 0.0).astype(BF16)

    def excl_cumsum(m):
        mf = jnp.where(m, 1.0, 0.0)
        carry = jnp.zeros((N_EXPERTS, 1), F32)
        parts = []
        for k in range(seq // blk):
            piece = mf[:, k * blk:(k + 1) * blk]
            parts.append(jnp.dot(piece.astype(BF16), before, preferred_element_type=F32) + carry)
            carry = carry + jnp.sum(piece, axis=-1, keepdims=True)
        return jnp.concatenate(parts, axis=-1)

    need = cap - count(gt)
    sel = gt | (eq & (excl_cumsum(eq) < need))
    slot = jnp.where(sel, excl_cumsum(sel), -1.0)
    slot_ref[0] = slot
    pad = jnp.full((LANES - N_EXPERTS, seq), -1.0, F32)
    slott_ref[0] = jnp.concatenate([slot, pad], axis=0).T

    j_iota = lax.broadcasted_iota(jnp.int32, (cap, seq), 0).astype(F32)

    def gate_of_slot(e, carry):
        hit = slot_ref[0, pl.ds(e, 1), :] == j_iota
        gslot_ref[0, e] = jnp.sum(jnp.where(hit, aff_ref[pl.ds(e, 1), :], 0.0), axis=-1, keepdims=True)
        return carry

    lax.fori_loop(0, N_EXPERTS, gate_of_slot, 0)


def _routing(logits, cap):
    bsz, seq, _ = logits.shape
    return pl.pallas_call(
        functools.partial(_routing_kernel, cap=cap),
        grid=(bsz,),
        in_specs=[pl.BlockSpec((1, seq, LANES), lambda b: (b, 0, 0))],
        out_specs=[pl.BlockSpec((1, N_EXPERTS, seq), lambda b: (b, 0, 0)),
                   pl.BlockSpec((1, seq, LANES), lambda b: (b, 0, 0)),
                   pl.BlockSpec((1, N_EXPERTS, cap, 1), lambda b: (b, 0, 0, 0))],
        out_shape=[jax.ShapeDtypeStruct((bsz, N_EXPERTS, seq), F32),
                   jax.ShapeDtypeStruct((bsz, seq, LANES), F32),
                   jax.ShapeDtypeStruct((bsz, N_EXPERTS, cap, 1), F32)],
        scratch_shapes=[pltpu.VMEM((N_EXPERTS, seq), F32)],
        compiler_params=_cparams(("arbitrary",)),
        name="routing",
    )(logits)


def _gather_kernel(slot_ref, x_ref, o_ref, pick_ref, *, cap):
    seq = x_ref.shape[1]
    td = x_ref.shape[2]

    @pl.when(pl.program_id(1) == 0)
    def _():
        j_iota = lax.broadcasted_iota(jnp.int32, (cap, seq), 0).astype(F32)

        def one_expert(e, carry):
            hit = slot_ref[0, pl.ds(e, 1), :] == j_iota
            pick_ref[pl.ds(pl.multiple_of(e * cap, cap), cap), :] = jnp.where(hit, 1.0, 0.0).astype(BF16)
            return carry

        lax.fori_loop(0, N_EXPERTS, one_expert, 0)

    rows = jnp.dot(pick_ref[...], x_ref[0].astype(BF16), preferred_element_type=F32)
    o_ref[...] = rows.reshape(N_EXPERTS, cap, td).astype(o_ref.dtype)


def _gather(slot, x1, cap, td):
    bsz, seq, d = x1.shape
    return pl.pallas_call(
        functools.partial(_gather_kernel, cap=cap),
        grid=(bsz, d // td),
        in_specs=[pl.BlockSpec((1, N_EXPERTS, seq), lambda b, j: (b, 0, 0)),
                  pl.BlockSpec((1, seq, td), lambda b, j: (b, 0, j))],
        out_specs=pl.BlockSpec((N_EXPERTS, cap, td), lambda b, j: (0, b, j)),
        out_shape=jax.ShapeDtypeStruct((N_EXPERTS, bsz * cap, d), BF16),
        scratch_shapes=[pltpu.VMEM((N_EXPERTS * cap, seq), BF16)],
        compiler_params=_cparams(("arbitrary", "arbitrary")),
        name="moe_gather",
    )(slot, x1)


def _gateup_kernel(xg_ref, wg_ref, wu_ref, h_ref):
    xg = xg_ref[0]
    g = jnp.dot(xg, wg_ref[0].astype(BF16), preferred_element_type=F32)
    u = jnp.dot(xg, wu_ref[0].astype(BF16), preferred_element_type=F32)
    h_ref[0] = (_silu(g) * u).astype(h_ref.dtype)


def _gateup(xg, w_gate_e, w_up_e, tf):
    n_e, rows, d = xg.shape
    ff = w_gate_e.shape[2]
    wspec = pl.BlockSpec((1, d, tf), lambda e, f: (e, 0, f))
    return pl.pallas_call(
        _gateup_kernel,
        grid=(n_e, ff // tf),
        in_specs=[pl.BlockSpec((1, rows, d), lambda e, f: (e, 0, 0)), wspec, wspec],
        out_specs=pl.BlockSpec((1, rows, tf), lambda e, f: (e, 0, f)),
        out_shape=jax.ShapeDtypeStruct((n_e, rows, ff), BF16),
        compiler_params=_cparams(("arbitrary", "arbitrary")),
        name="moe_gate_up",
    )(xg, w_gate_e, w_up_e)


def _down_kernel(h_ref, wd_ref, gs_ref, y_ref):
    y = jnp.dot(h_ref[0], wd_ref[0].astype(BF16), preferred_element_type=F32)
    y_ref[0] = (y * gs_ref[0]).astype(y_ref.dtype)


def _down(h, w_down_e, gslot, td):
    n_e, rows, ff = h.shape
    d = w_down_e.shape[2]
    return pl.pallas_call(
        _down_kernel,
        grid=(n_e, d // td),
        in_specs=[pl.BlockSpec((1, rows, ff), lambda e, j: (e, 0, 0)),
                  pl.BlockSpec((1, ff, td), lambda e, j: (e, 0, j)),
                  pl.BlockSpec((1, rows, 1), lambda e, j: (e, 0, 0))],
        out_specs=pl.BlockSpec((1, rows, td), lambda e, j: (e, 0, j)),
        out_shape=jax.ShapeDtypeStruct((n_e, rows, d), BF16),
        compiler_params=_cparams(("arbitrary", "arbitrary")),
        name="moe_down",
    )(h, w_down_e, gslot)


def _scatter_kernel(slott_ref, yg_ref, x1_ref, g_ref, b_ref, o_ref, put_ref, *, cap):
    dj = pl.program_id(2)
    ts = o_ref.shape[1]
    td = yg_ref.shape[2]
    n_slabs = o_ref.shape[2] // td

    @pl.when(dj == 0)
    def _():
        st = slott_ref[0]
        lane = lax.broadcasted_iota(jnp.int32, st.shape, 1)
        j_iota = lax.broadcasted_iota(jnp.int32, (ts, cap), 1).astype(F32)
        for e in range(N_EXPERTS):
            col = jnp.sum(jnp.where(lane == e, st, 0.0), axis=-1, keepdims=True)
            put_ref[:, e * cap:(e + 1) * cap] = jnp.where(col == j_iota, 1.0, 0.0).astype(BF16)

    part = jnp.dot(put_ref[...], yg_ref[...].reshape(N_EXPERTS * cap, td), preferred_element_type=F32)

    for slab in range(n_slabs):
        @pl.when(dj == slab)
        def _(slab=slab):
            o_ref[0, :, slab * td:(slab + 1) * td] = part

    @pl.when(dj == n_slabs - 1)
    def _():
        o_ref[0] = _layer_norm(ALPHA * x1_ref[0] + o_ref[0], g_ref[...], b_ref[...])


def _scatter_ln(slot_t, yg, x1, g, b, cap, ts, td):
    bsz, seq, d = x1.shape
    par = pl.BlockSpec((1, d), lambda bi, i, j: (0, 0))
    return pl.pallas_call(
        functools.partial(_scatter_kernel, cap=cap),
        grid=(bsz, seq // ts, d // td),
        in_specs=[pl.BlockSpec((1, ts, LANES), lambda bi, i, j: (bi, i, 0)),
                  pl.BlockSpec((N_EXPERTS, cap, td), lambda bi, i, j: (0, bi, j)),
                  pl.BlockSpec((1, ts, d), lambda bi, i, j: (bi, i, 0)),
                  par, par],
        out_specs=pl.BlockSpec((1, ts, d), lambda bi, i, j: (bi, i, 0)),
        out_shape=jax.ShapeDtypeStruct((bsz, seq, d), F32),
        scratch_shapes=[pltpu.VMEM((ts, N_EXPERTS * cap), BF16)],
        compiler_params=_cparams(("arbitrary", "arbitrary", "arbitrary"), VMEM_LIMIT_BIG_TILES),
        name="moe_scatter_ln2",
    )(slot_t, yg, x1, g, b)


def _group_heads(t):
    bsz, seq, _ = t.shape
    t = t[:, :, :2 * SSM_HEADS].reshape(bsz, seq, 2, SSM_GROUPS, HEADS_PER_GROUP)
    return jnp.transpose(t, (0, 3, 1, 2, 4)).reshape(bsz, SSM_GROUPS, seq, 2 * HEADS_PER_GROUP)


def _layer(x, w_in, b_gate, lq1, lk1, lq2, lk2, subln_w, conv_w, conv_b, dtb_f, dtb_b, alog_f, alog_b,
           d_skip, ssm_norm_w, w_branch, w_out, ln1_g, ln1_b, w_router, w_gate_e, w_up_e, w_down_e,
           ln2_g, ln2_b, layer_idx):
    bsz, seq, d = x.shape
    m = bsz * seq
    d_qk = ATTN_HEADS * 2 * ATTN_HEAD_DIM
    d_v = ATTN_HEADS * ATTN_V_DIM
    d_ssm = SSM_HEADS * SSM_HEAD_DIM
    d_conv = d_ssm + 2 * SSM_GROUPS * SSM_STATE
    n_main = 2 * d_qk + d_v + d_ssm + d_conv
    n_dt = 2 * SSM_HEADS
    lambda_init = 0.8 - 0.6 * math.exp(-0.3 * layer_idx)
    cap = CAPACITY_FACTOR * seq // N_EXPERTS
    row = lambda v: v.reshape(1, -1)

    xb = x.reshape(m, d).astype(BF16)
    w_in_t = w_in.T
    w_gate_t = w_in_t[n_main + n_dt:].astype(BF16)

    tn_main = 768
    proj = _matmul_wt(xb, w_in_t, 0, n_main // tn_main, F32, 1024, tn_main, "in_proj")
    proj = proj.reshape(bsz, seq, n_main)
    dt_raw = _matmul_wt(xb, w_in_t, n_main // LANES, 1, F32, 512, LANES, "dt_proj")
    dt_raw = dt_raw.reshape(bsz, seq, LANES)

    lane_pad = lambda a, bvec: jnp.pad(jnp.concatenate([a, bvec]), (0, LANES - n_dt)).reshape(1, LANES)
    dt, cs = _dtprep(dt_raw, lane_pad(dtb_f, dtb_b), lane_pad(alog_f, alog_b))
    dtg, csg = _group_heads(dt), _group_heads(cs)
    n_chunks = seq // SSM_CHUNK
    cstg = jnp.transpose(csg.reshape(bsz, SSM_GROUPS, n_chunks, SSM_CHUNK, 2 * HEADS_PER_GROUP),
                         (0, 1, 2, 4, 3))

    y_attn = _attention(proj, row(lq1), row(lk1), row(lq2), row(lk2), row(subln_w), lambda_init, 256)
    y_ssm = _ssd(proj, dtg, csg, cstg, conv_w, row(conv_b), row(jnp.repeat(d_skip, SSM_HEAD_DIM)),
                 row(ssm_norm_w), 2 * d_qk + d_v, 2 * d_qk + d_v + d_ssm)

    u = _merge(xb, w_gate_t, row(b_gate), y_attn.reshape(m, d_v), y_ssm.reshape(m, d_ssm),
               w_branch, 512, 512)
    w_router_p = jnp.pad(w_router, ((0, 0), (0, LANES - N_EXPERTS))).astype(BF16)
    x1, logits = _outproj_ln(u, w_out.astype(BF16), x.reshape(m, d), row(ln1_g), row(ln1_b),
                             w_router_p, 512, 512)
    x1 = x1.reshape(bsz, seq, d)

    slot, slot_t, gslot = _routing(logits.reshape(bsz, seq, LANES), cap)
    xg = _gather(slot, x1, cap, 512)
    h = _gateup(xg, w_gate_e, w_up_e, 256)
    gslot_e = jnp.transpose(gslot, (1, 0, 2, 3)).reshape(N_EXPERTS, bsz * cap, 1)
    yg = _down(h, w_down_e, gslot_e, 1024)
    return _scatter_ln(slot_t, yg, x1, row(ln2_g), row(ln2_b), cap, 512, 512)


def kernel(x, w_in, b_gate, lambda_q1, lambda_k1, lambda_q2, lambda_k2, attn_subln_w, conv_w, conv_b,
           dt_bias_fwd, dt_bias_bwd, a_log_fwd, a_log_bwd, d_skip, ssm_norm_w, w_branch, w_out,
           ln1_g, ln1_b, w_router, w_gate_e, w_up_e, w_down_e, ln2_g, ln2_b):
    for l in range(w_in.shape[0]):
        x = _layer(x, w_in[l], b_gate[l], lambda_q1[l], lambda_k1[l], lambda_q2[l], lambda_k2[l],
                   attn_subln_w[l], conv_w[l], conv_b[l], dt_bias_fwd[l], dt_bias_bwd[l],
                   a_log_fwd[l], a_log_bwd[l], d_skip[l], ssm_norm_w[l], w_branch[l], w_out[l],
                   ln1_g[l], ln1_b[l], w_router[l], w_gate_e[l], w_up_e[l], w_down_e[l],
                   ln2_g[l], ln2_b[l], l)
    return x
```

```python
import functools
import math

import jax
import jax.numpy as jnp
from jax import lax
from jax.experimental import pallas as pl
from jax.experimental.pallas import tpu as pltpu

F32 = jnp.float32
BF16 = jnp.bfloat16

ATTN_HEADS = 16
ATTN_HEAD_DIM = 64
ATTN_V_DIM = 128
ROPE_THETA = 10000.0
LOG2_E = math.log2(math.e)
SSM_HEAD_DIM = 64
SSM_HEADS = 32
SSM_GROUPS = 8
SSM_STATE = 128
SSM_CHUNK = 128
CONV_WIDTH = 5
N_EXPERTS = 16
CAPACITY_FACTOR = 2
DEPTH = 1
ALPHA = (2.0 * DEPTH) ** 0.25

LANES = 128
SUBLANES = 8
VMEM_LIMIT = 56 * 1024 * 1024
VMEM_LIMIT_BIG_TILES = 62 * 1024 * 1024

HEADS_PER_GROUP = SSM_HEADS // SSM_GROUPS
GROUP_W = HEADS_PER_GROUP * SSM_HEAD_DIM
CONV_PAD_ROWS = SUBLANES


def _cparams(sem, vmem_limit=VMEM_LIMIT):
    return pltpu.CompilerParams(dimension_semantics=sem, vmem_limit_bytes=vmem_limit)


def _sigmoid(x):
    return 1.0 / (1.0 + jnp.exp(-x))


def _silu(x):
    return x * _sigmoid(x)


def _softplus(x):
    return jnp.maximum(x, 0.0) + jnp.log1p(jnp.exp(-jnp.abs(x)))


def _layer_norm(r, g, b):
    mu = jnp.mean(r, axis=-1, keepdims=True)
    d = r - mu
    var = jnp.mean(d * d, axis=-1, keepdims=True)
    return d * lax.rsqrt(var + 1e-5) * g + b


def _mm_wt_kernel(a_ref, w_ref, o_ref, wb_ref):
    @pl.when(pl.program_id(1) == 0)
    def _():
        wb_ref[...] = w_ref[...].astype(BF16)

    o_ref[...] = lax.dot_general(a_ref[...], wb_ref[...], (((1,), (1,)), ((), ())),
                                 preferred_element_type=F32).astype(o_ref.dtype)


def _matmul_wt(a, w_t, blk0, n_blk, out_dtype, tm, tn, name):
    m, k = a.shape
    return pl.pallas_call(
        _mm_wt_kernel,
        grid=(n_blk, m // tm),
        in_specs=[pl.BlockSpec((tm, k), lambda j, i: (i, 0)),
                  pl.BlockSpec((tn, k), lambda j, i: (blk0 + j, 0))],
        out_specs=pl.BlockSpec((tm, tn), lambda j, i: (i, j)),
        out_shape=jax.ShapeDtypeStruct((m, n_blk * tn), out_dtype),
        scratch_shapes=[pltpu.VMEM((tn, k), BF16)],
        compiler_params=_cparams(("arbitrary", "arbitrary"), VMEM_LIMIT_BIG_TILES),
        name=name,
    )(a, w_t)


def _xcast_dt_kernel(x_ref, w_ref, xb_ref, dt_ref):
    xb = x_ref[...].astype(BF16)
    xb_ref[...] = xb
    dt_ref[...] = lax.dot_general(xb, w_ref[...].astype(BF16), (((1,), (1,)), ((), ())),
                                  preferred_element_type=F32)


def _xcast_dt(x, w_t, blk):
    m, k = x.shape
    tm = 512
    return pl.pallas_call(
        _xcast_dt_kernel,
        grid=(m // tm,),
        in_specs=[pl.BlockSpec((tm, k), lambda i: (i, 0)),
                  pl.BlockSpec((LANES, k), lambda i: (blk, 0))],
        out_specs=[pl.BlockSpec((tm, k), lambda i: (i, 0)),
                   pl.BlockSpec((tm, LANES), lambda i: (i, 0))],
        out_shape=[jax.ShapeDtypeStruct((m, k), BF16), jax.ShapeDtypeStruct((m, LANES), F32)],
        compiler_params=_cparams(("arbitrary",)),
        name="x_cast_dt_proj",
    )(x, w_t)


def _dtprep_kernel(raw_ref, bias_ref, alog_ref, dt_ref, cs_ref):
    seq = raw_ref.shape[1]
    lc = SSM_CHUNK
    dt = _softplus(raw_ref[0] + bias_ref[...])
    dt_ref[0] = dt
    la = dt * (-jnp.exp(alog_ref[...]))
    row = lax.broadcasted_iota(jnp.int32, (lc, lc), 0)
    col = lax.broadcasted_iota(jnp.int32, (lc, lc), 1)
    t_low = jnp.where(row >= col, 1.0, 0.0).astype(F32)
    t_up = jnp.where(row <= col, 1.0, 0.0).astype(F32)
    fwd_lane = lax.broadcasted_iota(jnp.int32, (1, LANES), 1) < SSM_HEADS
    for c in range(seq // lc):
        lac = la[c * lc:(c + 1) * lc]
        f = jnp.dot(t_low, lac, preferred_element_type=F32, precision=lax.Precision.HIGHEST)
        b = jnp.dot(t_up, lac, preferred_element_type=F32, precision=lax.Precision.HIGHEST)
        cs_ref[0, c * lc:(c + 1) * lc, :] = jnp.where(fwd_lane, f, b)


def _dtprep(raw, bias, alog):
    bsz, seq, _ = raw.shape
    blk = pl.BlockSpec((1, seq, LANES), lambda b: (b, 0, 0))
    par = pl.BlockSpec((1, LANES), lambda b: (0, 0))
    return pl.pallas_call(
        _dtprep_kernel,
        grid=(bsz,),
        in_specs=[blk, par, par],
        out_specs=[blk, blk],
        out_shape=[jax.ShapeDtypeStruct(raw.shape, F32)] * 2,
        compiler_params=_cparams(("arbitrary",)),
        name="dt_prep",
    )(raw, bias, alog)


def _attn_kernel(lq1_ref, lk1_ref, lq2_ref, lk2_ref, sw_ref, cos_ref, sin_ref,
                 q_ref, k_ref, v_ref, o_ref, kr_ref, vb_ref, qs_ref, sa_ref, sb_ref, *, lambda_init, tq):
    seq = q_ref.shape[1]
    n_tiles = seq // tq
    lane = lax.broadcasted_iota(jnp.int32, (1, LANES), 1)
    first_half = (lane & (ATTN_HEAD_DIM // 2)) == 0
    comp1 = lane < ATTN_HEAD_DIM

    def rope(x, c, s):
        partner = jnp.where(first_half,
                            pltpu.roll(x, LANES - ATTN_HEAD_DIM // 2, 1),
                            pltpu.roll(x, ATTN_HEAD_DIM // 2, 1))
        return x * c + partner * s

    kr_ref[...] = rope(k_ref[0], cos_ref[...], sin_ref[...]).astype(BF16)
    vb_ref[:, 0:ATTN_V_DIM] = v_ref[0].astype(BF16)
    vb_ref[:, ATTN_V_DIM:] = jnp.where(lane == 0, 1.0, 0.0).astype(BF16) * jnp.ones((seq, 1), BF16)
    q = rope(q_ref[0], cos_ref[...], sin_ref[...]) * (ATTN_HEAD_DIM ** -0.5 * LOG2_E)
    for t in range(n_tiles):
        qt = q[t * tq:(t + 1) * tq]
        qs_ref[t, 0:tq, :] = jnp.where(comp1, qt, 0.0).astype(BF16)
        qs_ref[t, tq:2 * tq, :] = jnp.where(comp1, 0.0, qt).astype(BF16)

    lam = (jnp.exp(jnp.sum(lq1_ref[...] * lk1_ref[...], axis=-1, keepdims=True))
           - jnp.exp(jnp.sum(lq2_ref[...] * lk2_ref[...], axis=-1, keepdims=True)) + lambda_init)

    def scores(t, dst_ref):
        dst_ref[...] = lax.dot_general(qs_ref[t], kr_ref[...], (((1,), (1,)), ((), ())),
                                       preferred_element_type=F32)

    def finish(t, src_ref):
        s = src_ref[...]
        p = jnp.exp2(s - jnp.max(s, axis=-1, keepdims=True)).astype(BF16)
        pv = jnp.dot(p, vb_ref[...], preferred_element_type=F32)
        inv = 1.0 / pv[:, ATTN_V_DIM:ATTN_V_DIM + 1]
        o = pv[:tq, :ATTN_V_DIM] * inv[:tq] - pv[tq:, :ATTN_V_DIM] * (lam * inv[tq:])
        o = o * lax.rsqrt(jnp.mean(o * o, axis=-1, keepdims=True) + 1e-6) * sw_ref[...]
        o_ref[0, pl.ds(pl.multiple_of(t * tq, tq), tq), :] = (o * (1.0 - lambda_init)).astype(o_ref.dtype)

    scores(0, sa_ref)

    def pair(j, carry):
        t = 2 * j
        scores(t + 1, sb_ref)
        finish(t, sa_ref)
        scores(t + 2, sa_ref)
        finish(t + 1, sb_ref)
        return carry

    lax.fori_loop(0, n_tiles // 2 - 1, pair, 0)
    scores(n_tiles - 1, sb_ref)
    finish(n_tiles - 2, sa_ref)
    finish(n_tiles - 1, sb_ref)


def _rope_tables(seq):
    half = ATTN_HEAD_DIM // 2
    inv_freq = ROPE_THETA ** (-jnp.arange(0, ATTN_HEAD_DIM, 2, dtype=F32) / ATTN_HEAD_DIM)
    ang = jnp.arange(seq, dtype=F32)[:, None] * inv_freq[None, :]
    cos, sin = jnp.cos(ang), jnp.sin(ang)
    reps = LANES // half
    sign = jnp.tile(jnp.concatenate([-jnp.ones((half,), F32), jnp.ones((half,), F32)]), reps // 2)
    return jnp.tile(cos, (1, reps)), jnp.tile(sin, (1, reps)) * sign[None, :]


def _attention(proj, lq1, lk1, lq2, lk2, subln_w, lambda_init, tq):
    bsz, seq, _ = proj.shape
    cos, sin = _rope_tables(seq)
    assert seq % (2 * tq) == 0 and seq // tq >= 2
    vec = pl.BlockSpec((1, ATTN_HEAD_DIM), lambda b, h: (0, 0))
    table = pl.BlockSpec((seq, LANES), lambda b, h: (0, 0))
    return pl.pallas_call(
        functools.partial(_attn_kernel, lambda_init=lambda_init, tq=tq),
        grid=(bsz, ATTN_HEADS),
        in_specs=[vec, vec, vec, vec,
                  pl.BlockSpec((1, LANES), lambda b, h: (0, 0)),
                  table, table,
                  pl.BlockSpec((1, seq, LANES), lambda b, h: (b, 0, h)),
                  pl.BlockSpec((1, seq, LANES), lambda b, h: (b, 0, ATTN_HEADS + h)),
                  pl.BlockSpec((1, seq, LANES), lambda b, h: (b, 0, 2 * ATTN_HEADS + h))],
        out_specs=pl.BlockSpec((1, seq, LANES), lambda b, h: (b, 0, h)),
        out_shape=jax.ShapeDtypeStruct((bsz, seq, ATTN_HEADS * ATTN_V_DIM), BF16),
        scratch_shapes=[pltpu.VMEM((seq, LANES), BF16), pltpu.VMEM((seq, ATTN_V_DIM + LANES), BF16),
                        pltpu.VMEM((seq // tq, 2 * tq, LANES), BF16),
                        pltpu.VMEM((2 * tq, seq), F32), pltpu.VMEM((2 * tq, seq), F32)],
        compiler_params=_cparams(("arbitrary", "arbitrary")),
        name="diff_attention",
    )(lq1, lk1, lq2, lk2, subln_w, cos, sin, proj, proj, proj)


def _ssd_kernel(z_ref, x_ref, b_ref, c_ref, dt_ref, cs_ref, cst_ref,
                cwx_ref, cwb_ref, cwc_ref, cbx_ref, cbb_ref, cbc_ref, dsk_ref, nw_ref,
                o_ref, pad_ref, xs_ref, bm_ref, cm_ref, y_ref, h_ref):
    seq = x_ref.shape[1]
    lc = SSM_CHUNK
    n_chunks = seq // lc
    row_tile = 256

    def conv_silu(in_ref, w_ref, bias_ref, out_ref, width):
        zeros = jnp.zeros((CONV_PAD_ROWS, width), F32)
        pad_ref[0:CONV_PAD_ROWS, 0:width] = zeros
        pad_ref[CONV_PAD_ROWS + seq:2 * CONV_PAD_ROWS + seq, 0:width] = zeros
        pad_ref[CONV_PAD_ROWS:CONV_PAD_ROWS + seq, 0:width] = in_ref[0]
        half = (CONV_WIDTH - 1) // 2
        for t in range(seq // row_tile):
            acc = jnp.broadcast_to(bias_ref[...], (row_tile, width))
            for j in range(CONV_WIDTH):
                start = CONV_PAD_ROWS + t * row_tile + j - half
                acc = acc + pad_ref[start:start + row_tile, 0:width] * w_ref[j:j + 1, :]
            out_ref[t * row_tile:(t + 1) * row_tile, :] = _silu(acc)

    conv_silu(x_ref, cwx_ref, cbx_ref, xs_ref, GROUP_W)
    conv_silu(b_ref, cwb_ref, cbb_ref, bm_ref, SSM_STATE)
    conv_silu(c_ref, cwc_ref, cbc_ref, cm_ref, SSM_STATE)

    head_of_lane = lax.broadcasted_iota(jnp.int32, (1, GROUP_W), 1) // SSM_HEAD_DIM
    row = lax.broadcasted_iota(jnp.int32, (lc, lc), 0)
    col = lax.broadcasted_iota(jnp.int32, (lc, lc), 1)

    def expand(cols, off):
        out = cols[:, off + HEADS_PER_GROUP - 1:off + HEADS_PER_GROUP]
        for r in range(HEADS_PER_GROUP - 2, -1, -1):
            out = jnp.where(head_of_lane == r, cols[:, off + r:off + r + 1], out)
        return out

    def run_direction(reverse):
        off = HEADS_PER_GROUP if reverse else 0
        mask = (row <= col) if reverse else (row >= col)
        edge = 0 if reverse else lc - 1
        h_ref[...] = jnp.zeros_like(h_ref)

        def body(ci, carry):
            c = (n_chunks - 1 - ci) if reverse else ci
            r0 = pl.multiple_of(c * lc, lc)
            xc = xs_ref[pl.ds(r0, lc), :]
            bc = bm_ref[pl.ds(r0, lc), :]
            cc = cm_ref[pl.ds(r0, lc), :].astype(BF16)
            dtc = dt_ref[0, 0, pl.ds(r0, lc), :]
            csc = cs_ref[0, 0, pl.ds(r0, lc), :]
            cst = cst_ref[0, 0, c]
            cs_e = expand(csc, off)
            edge_e = expand(csc[edge:edge + 1, :], off)
            xdt = xc * expand(dtc, off)
            xdt_b = xdt.astype(BF16)
            cb = lax.dot_general(cc, bc.astype(BF16), (((1,), (1,)), ((), ())),
                                 preferred_element_type=F32)
            y = jnp.zeros((lc, GROUP_W), F32)
            for r in range(HEADS_PER_GROUP):
                diff = csc[:, off + r:off + r + 1] - cst[off + r:off + r + 1, :]
                decay = jnp.exp(jnp.where(mask, diff, -jnp.inf))
                yr = jnp.dot((cb * decay).astype(BF16), xdt_b, preferred_element_type=F32)
                y = jnp.where(head_of_lane == r, yr, y)
            h_t = h_ref[...]
            y = y + jnp.dot(cc, h_t.astype(BF16), preferred_element_type=F32) * jnp.exp(cs_e)
            new_state = jnp.dot(bc.T.astype(BF16), (xdt * jnp.exp(edge_e - cs_e)).astype(BF16),
                                preferred_element_type=F32)
            h_ref[...] = h_t * jnp.exp(edge_e) + new_state
            if reverse:
                y_ref[pl.ds(r0, lc), :] += y
            else:
                y_ref[pl.ds(r0, lc), :] = y
            return carry

        lax.fori_loop(0, n_chunks, body, 0, unroll=8)

    run_direction(False)
    run_direction(True)

    for t in range(seq // row_tile):
        rows = slice(t * row_tile, (t + 1) * row_tile)
        y = y_ref[rows, :] + xs_ref[rows, :] * dsk_ref[...]
        y = y * _silu(z_ref[0, rows, :])
        y = y * lax.rsqrt(jnp.mean(y * y, axis=-1, keepdims=True) + 1e-6) * nw_ref[...]
        o_ref[0, rows, :] = y.astype(o_ref.dtype)


def _ssd(proj, dtg, csg, cstg, conv_w, conv_b, dskip_e, norm_w, col0_z, col0_xbc):
    bsz, seq, _ = proj.shape
    d_ssm = SSM_HEADS * SSM_HEAD_DIM
    gn = SSM_GROUPS * SSM_STATE
    n_chunks = seq // SSM_CHUNK
    zb, xb = col0_z // GROUP_W, col0_xbc // GROUP_W
    bb, cb = (col0_xbc + d_ssm) // SSM_STATE, (col0_xbc + d_ssm + gn) // SSM_STATE
    wb, wc = d_ssm // SSM_STATE, (d_ssm + gn) // SSM_STATE
    n_dir_heads = 2 * HEADS_PER_GROUP
    return pl.pallas_call(
        _ssd_kernel,
        grid=(bsz, SSM_GROUPS),
        in_specs=[pl.BlockSpec((1, seq, GROUP_W), lambda b, g: (b, 0, zb + g)),
                  pl.BlockSpec((1, seq, GROUP_W), lambda b, g: (b, 0, xb + g)),
                  pl.BlockSpec((1, seq, SSM_STATE), lambda b, g: (b, 0, bb + g)),
                  pl.BlockSpec((1, seq, SSM_STATE), lambda b, g: (b, 0, cb + g)),
                  pl.BlockSpec((1, 1, seq, n_dir_heads), lambda b, g: (b, g, 0, 0)),
                  pl.BlockSpec((1, 1, seq, n_dir_heads), lambda b, g: (b, g, 0, 0)),
                  pl.BlockSpec((1, 1, n_chunks, n_dir_heads, SSM_CHUNK), lambda b, g: (b, g, 0, 0, 0)),
                  pl.BlockSpec((CONV_WIDTH, GROUP_W), lambda b, g: (0, g)),
                  pl.BlockSpec((CONV_WIDTH, SSM_STATE), lambda b, g: (0, wb + g)),
                  pl.BlockSpec((CONV_WIDTH, SSM_STATE), lambda b, g: (0, wc + g)),
                  pl.BlockSpec((1, GROUP_W), lambda b, g: (0, g)),
                  pl.BlockSpec((1, SSM_STATE), lambda b, g: (0, wb + g)),
                  pl.BlockSpec((1, SSM_STATE), lambda b, g: (0, wc + g)),
                  pl.BlockSpec((1, GROUP_W), lambda b, g: (0, g)),
                  pl.BlockSpec((1, GROUP_W), lambda b, g: (0, g))],
        out_specs=pl.BlockSpec((1, seq, GROUP_W), lambda b, g: (b, 0, g)),
        out_shape=jax.ShapeDtypeStruct((bsz, seq, d_ssm), BF16),
        scratch_shapes=[pltpu.VMEM((seq + 2 * CONV_PAD_ROWS, GROUP_W), F32),
                        pltpu.VMEM((seq, GROUP_W), F32),
                        pltpu.VMEM((seq, SSM_STATE), F32),
                        pltpu.VMEM((seq, SSM_STATE), F32),
                        pltpu.VMEM((seq, GROUP_W), F32),
                        pltpu.VMEM((SSM_STATE, GROUP_W), F32)],
        compiler_params=_cparams(("arbitrary", "arbitrary")),
        name="ssd",
    )(proj, proj, proj, proj, dtg, csg, cstg, conv_w, conv_w, conv_w, conv_b, conv_b, conv_b,
      dskip_e, norm_w)


def _merge_kernel(x_ref, wga_ref, wgs_ref, bga_ref, bgs_ref, ya_ref, ys_ref, wba_ref, wbs_ref, u_ref,
                  wba_b_ref, wbs_b_ref):
    @pl.when(pl.program_id(1) == 0)
    def _():
        wba_b_ref[...] = wba_ref[...].astype(BF16)
        wbs_b_ref[...] = wbs_ref[...].astype(BF16)

    x = x_ref[...]
    nt = (((1,), (1,)), ((), ()))
    ga = _sigmoid(lax.dot_general(x, wga_ref[...], nt, preferred_element_type=F32) + bga_ref[...])
    gs = _sigmoid(lax.dot_general(x, wgs_ref[...], nt, preferred_element_type=F32) + bgs_ref[...])
    ba = jnp.dot(ya_ref[...], wba_b_ref[...], preferred_element_type=F32)
    bs = jnp.dot(ys_ref[...], wbs_b_ref[...], preferred_element_type=F32)
    u_ref[...] = (ga * ba + gs * bs).astype(u_ref.dtype)


def _merge(xb, w_gate, b_gate, y_attn, y_ssm, w_branch, tm, tn):
    m, d = xb.shape
    n = w_branch.shape[1]
    ka, ks = y_attn.shape[1], y_ssm.shape[1]
    nj = n // tn
    return pl.pallas_call(
        _merge_kernel,
        grid=(nj, m // tm),
        in_specs=[pl.BlockSpec((tm, d), lambda j, i: (i, 0)),
                  pl.BlockSpec((tn, d), lambda j, i: (j, 0)),
                  pl.BlockSpec((tn, d), lambda j, i: (nj + j, 0)),
                  pl.BlockSpec((1, tn), lambda j, i: (0, j)),
                  pl.BlockSpec((1, tn), lambda j, i: (0, nj + j)),
                  pl.BlockSpec((tm, ka), lambda j, i: (i, 0)),
                  pl.BlockSpec((tm, ks), lambda j, i: (i, 0)),
                  pl.BlockSpec((ka, tn), lambda j, i: (0, j)),
                  pl.BlockSpec((ks, tn), lambda j, i: (ka // ks, j))],
        out_specs=pl.BlockSpec((tm, tn), lambda j, i: (i, j)),
        out_shape=jax.ShapeDtypeStruct((m, n), BF16),
        scratch_shapes=[pltpu.VMEM((ka, tn), BF16), pltpu.VMEM((ks, tn), BF16)],
        compiler_params=_cparams(("arbitrary", "arbitrary"), VMEM_LIMIT_BIG_TILES),
        name="gated_merge",
    )(xb, w_gate, w_gate, b_gate, b_gate, y_attn, y_ssm, w_branch, w_branch)


def _outproj_kernel(u_ref, w_ref, x_ref, g_ref, b_ref, wr_ref, x1_ref, lg_ref):
    j = pl.program_id(1)
    tn = w_ref.shape[1]
    n_slabs = x1_ref.shape[1] // tn
    part = jnp.dot(u_ref[...], w_ref[...], preferred_element_type=F32)

    for slab in range(n_slabs):
        @pl.when(j == slab)
        def _(slab=slab):
            x1_ref[:, slab * tn:(slab + 1) * tn] = part

    @pl.when(j == n_slabs - 1)
    def _():
        x1 = _layer_norm(ALPHA * x_ref[...] + x1_ref[...], g_ref[...], b_ref[...])
        x1_ref[...] = x1
        lg_ref[...] = jnp.dot(x1.astype(BF16), wr_ref[...], preferred_element_type=F32)


def _outproj_ln(u, w_out, x, g, b, w_router_p, tm, tn):
    m, kdim = u.shape
    d = w_out.shape[1]
    row = pl.BlockSpec((tm, d), lambda i, j: (i, 0))
    par = pl.BlockSpec((1, d), lambda i, j: (0, 0))
    return pl.pallas_call(
        _outproj_kernel,
        grid=(m // tm, d // tn),
        in_specs=[pl.BlockSpec((tm, kdim), lambda i, j: (i, 0)),
                  pl.BlockSpec((kdim, tn), lambda i, j: (0, j)),
                  pl.BlockSpec((tm, d), lambda i, j: (i, 0)),
                  par, par,
                  pl.BlockSpec((d, LANES), lambda i, j: (0, 0), pipeline_mode=pl.Buffered(1))],
        out_specs=[row, pl.BlockSpec((tm, LANES), lambda i, j: (i, 0))],
        out_shape=[jax.ShapeDtypeStruct((m, d), F32), jax.ShapeDtypeStruct((m, LANES), F32)],
        compiler_params=_cparams(("arbitrary", "arbitrary"), VMEM_LIMIT_BIG_TILES),
        name="out_proj_ln1",
    )(u, w_out, x, g, b, w_router_p)


def _routing_kernel(lg_ref, slot_ref, slott_ref, gslot_ref, aff_ref, *, cap):
    seq = lg_ref.shape[1]
    blk = 256
    lg = lg_ref[0]
    valid = lax.broadcasted_iota(jnp.int32, lg.shape, 1) < N_EXPERTS
    lgm = jnp.where(valid, lg, -jnp.inf)
    ex = jnp.exp(lgm - jnp.max(lgm, axis=-1, keepdims=True))
    aff = ex / jnp.sum(ex, axis=-1, keepdims=True)
    aff_t = aff.T[0:N_EXPERTS]
    aff_ref[...] = aff_t
    bits = lax.bitcast_convert_type(aff_t, jnp.int32)

    def count(m):
        return jnp.sum(jnp.where(m, 1.0, 0.0), axis=-1, keepdims=True)

    def search(i, thr):
        cand = thr | jnp.left_shift(jnp.int32(1), 30 - i)
        return jnp.where(count(bits >= cand) >= cap, cand, thr)

    thr = lax.fori_loop(0, 31, search, jnp.zeros((N_EXPERTS, 1), jnp.int32))
    gt = bits > thr
    eq = bits == thr

    r_i = lax.broadcasted_iota(jnp.int32, (blk, blk), 0)
    c_i = lax.broadcasted_iota(jnp.int32, (blk, blk), 1)
    before = jnp.where(r_i < c_i, 1.0, 0.0).astype(BF16)

    def excl_cumsum(m):
        mf = jnp.where(m, 1.0, 0.0)
        carry = jnp.zeros((N_EXPERTS, 1), F32)
        parts = []
        for k in range(seq // blk):
            piece = mf[:, k * blk:(k + 1) * blk]
            parts.append(jnp.dot(piece.astype(BF16), before, preferred_element_type=F32) + carry)
            carry = carry + jnp.sum(piece, axis=-1, keepdims=True)
        return jnp.concatenate(parts, axis=-1)

    need = cap - count(gt)
    sel = gt | (eq & (excl_cumsum(eq) < need))
    slot = jnp.where(sel, excl_cumsum(sel), -1.0)
    slot_ref[0] = slot
    pad = jnp.full((LANES - N_EXPERTS, seq), -1.0, F32)
    slott_ref[0] = jnp.concatenate([slot, pad], axis=0).T

    j_iota = lax.broadcasted_iota(jnp.int32, (cap, seq), 0).astype(F32)

    def gate_of_slot(e, carry):
        hit = slot_ref[0, pl.ds(e, 1), :] == j_iota
        gslot_ref[0, e] = jnp.sum(jnp.where(hit, aff_ref[pl.ds(e, 1), :], 0.0), axis=-1, keepdims=True)
        return carry

    lax.fori_loop(0, N_EXPERTS, gate_of_slot, 0)


def _routing(logits, cap):
    bsz, seq, _ = logits.shape
    return pl.pallas_call(
        functools.partial(_routing_kernel, cap=cap),
        grid=(bsz,),
        in_specs=[pl.BlockSpec((1, seq, LANES), lambda b: (b, 0, 0))],
        out_specs=[pl.BlockSpec((1, N_EXPERTS, seq), lambda b: (b, 0, 0)),
                   pl.BlockSpec((1, seq, LANES), lambda b: (b, 0, 0)),
                   pl.BlockSpec((1, N_EXPERTS, cap, 1), lambda b: (b, 0, 0, 0))],
        out_shape=[jax.ShapeDtypeStruct((bsz, N_EXPERTS, seq), F32),
                   jax.ShapeDtypeStruct((bsz, seq, LANES), F32),
                   jax.ShapeDtypeStruct((bsz, N_EXPERTS, cap, 1), F32)],
        scratch_shapes=[pltpu.VMEM((N_EXPERTS, seq), F32)],
        compiler_params=_cparams(("arbitrary",)),
        name="routing",
    )(logits)


def _gather_kernel(slot_ref, x_ref, o_ref, pick_ref, *, cap):
    seq = x_ref.shape[1]
    td = x_ref.shape[2]

    @pl.when(pl.program_id(1) == 0)
    def _():
        j_iota = lax.broadcasted_iota(jnp.int32, (cap, seq), 0).astype(F32)

        def one_expert(e, carry):
            hit = slot_ref[0, pl.ds(e, 1), :] == j_iota
            pick_ref[pl.ds(pl.multiple_of(e * cap, cap), cap), :] = jnp.where(hit, 1.0, 0.0).astype(BF16)
            return carry

        lax.fori_loop(0, N_EXPERTS, one_expert, 0)

    rows = jnp.dot(pick_ref[...], x_ref[0].astype(BF16), preferred_element_type=F32)
    o_ref[...] = rows.reshape(N_EXPERTS, cap, td).astype(o_ref.dtype)


def _gather(slot, x1, cap, td):
    bsz, seq, d = x1.shape
    return pl.pallas_call(
        functools.partial(_gather_kernel, cap=cap),
        grid=(bsz, d // td),
        in_specs=[pl.BlockSpec((1, N_EXPERTS, seq), lambda b, j: (b, 0, 0)),
                  pl.BlockSpec((1, seq, td), lambda b, j: (b, 0, j))],
        out_specs=pl.BlockSpec((N_EXPERTS, cap, td), lambda b, j: (0, b, j)),
        out_shape=jax.ShapeDtypeStruct((N_EXPERTS, bsz * cap, d), BF16),
        scratch_shapes=[pltpu.VMEM((N_EXPERTS * cap, seq), BF16)],
        compiler_params=_cparams(("arbitrary", "arbitrary")),
        name="moe_gather",
    )(slot, x1)


def _gateup_kernel(xg_ref, wg_ref, wu_ref, h_ref):
    xg = xg_ref[0]
    g = jnp.dot(xg, wg_ref[0].astype(BF16), preferred_element_type=F32)
    u = jnp.dot(xg, wu_ref[0].astype(BF16), preferred_element_type=F32)
    h_ref[0] = (_silu(g) * u).astype(h_ref.dtype)


def _gateup(xg, w_gate_e, w_up_e, tf):
    n_e, rows, d = xg.shape
    ff = w_gate_e.shape[2]
    wspec = pl.BlockSpec((1, d, tf), lambda e, f: (e, 0, f))
    return pl.pallas_call(
        _gateup_kernel,
        grid=(n_e, ff // tf),
        in_specs=[pl.BlockSpec((1, rows, d), lambda e, f: (e, 0, 0)), wspec, wspec],
        out_specs=pl.BlockSpec((1, rows, tf), lambda e, f: (e, 0, f)),
        out_shape=jax.ShapeDtypeStruct((n_e, rows, ff), BF16),
        compiler_params=_cparams(("arbitrary", "arbitrary")),
        name="moe_gate_up",
    )(xg, w_gate_e, w_up_e)


def _down_kernel(h_ref, wd_ref, gs_ref, y_ref):
    y = jnp.dot(h_ref[0], wd_ref[0].astype(BF16), preferred_element_type=F32)
    y_ref[0] = (y * gs_ref[0]).astype(y_ref.dtype)


def _down(h, w_down_e, gslot, td):
    n_e, rows, ff = h.shape
    d = w_down_e.shape[2]
    return pl.pallas_call(
        _down_kernel,
        grid=(n_e, d // td),
        in_specs=[pl.BlockSpec((1, rows, ff), lambda e, j: (e, 0, 0)),
                  pl.BlockSpec((1, ff, td), lambda e, j: (e, 0, j)),
                  pl.BlockSpec((1, rows, 1), lambda e, j: (e, 0, 0))],
        out_specs=pl.BlockSpec((1, rows, td), lambda e, j: (e, 0, j)),
        out_shape=jax.ShapeDtypeStruct((n_e, rows, d), BF16),
        compiler_params=_cparams(("arbitrary", "arbitrary")),
        name="moe_down",
    )(h, w_down_e, gslot)


def _scatter_kernel(slott_ref, yg_ref, x1_ref, g_ref, b_ref, o_ref, put_ref, *, cap):
    dj = pl.program_id(2)
    ts = o_ref.shape[1]
    td = yg_ref.shape[2]
    n_slabs = o_ref.shape[2] // td

    @pl.when(dj == 0)
    def _():
        st = slott_ref[0]
        lane = lax.broadcasted_iota(jnp.int32, st.shape, 1)
        j_iota = lax.broadcasted_iota(jnp.int32, (ts, cap), 1).astype(F32)
        for e in range(N_EXPERTS):
            col = jnp.sum(jnp.where(lane == e, st, 0.0), axis=-1, keepdims=True)
            put_ref[:, e * cap:(e + 1) * cap] = jnp.where(col == j_iota, 1.0, 0.0).astype(BF16)

    part = jnp.dot(put_ref[...], yg_ref[...].reshape(N_EXPERTS * cap, td), preferred_element_type=F32)

    for slab in range(n_slabs):
        @pl.when(dj == slab)
        def _(slab=slab):
            o_ref[0, :, slab * td:(slab + 1) * td] = part

    @pl.when(dj == n_slabs - 1)
    def _():
        o_ref[0] = _layer_norm(ALPHA * x1_ref[0] + o_ref[0], g_ref[...], b_ref[...])


def _scatter_ln(slot_t, yg, x1, g, b, cap, ts, td):
    bsz, seq, d = x1.shape
    par = pl.BlockSpec((1, d), lambda bi, i, j: (0, 0))
    return pl.pallas_call(
        functools.partial(_scatter_kernel, cap=cap),
        grid=(bsz, seq // ts, d // td),
        in_specs=[pl.BlockSpec((1, ts, LANES), lambda bi, i, j: (bi, i, 0)),
                  pl.BlockSpec((N_EXPERTS, cap, td), lambda bi, i, j: (0, bi, j)),
                  pl.BlockSpec((1, ts, d), lambda bi, i, j: (bi, i, 0)),
                  par, par],
        out_specs=pl.BlockSpec((1, ts, d), lambda bi, i, j: (bi, i, 0)),
        out_shape=jax.ShapeDtypeStruct((bsz, seq, d), F32),
        scratch_shapes=[pltpu.VMEM((ts, N_EXPERTS * cap), BF16)],
        compiler_params=_cparams(("arbitrary", "arbitrary", "arbitrary"), VMEM_LIMIT_BIG_TILES),
        name="moe_scatter_ln2",
    )(slot_t, yg, x1, g, b)


def _group_heads(t):
    bsz, seq, _ = t.shape
    t = t[:, :, :2 * SSM_HEADS].reshape(bsz, seq, 2, SSM_GROUPS, HEADS_PER_GROUP)
    return jnp.transpose(t, (0, 3, 1, 2, 4)).reshape(bsz, SSM_GROUPS, seq, 2 * HEADS_PER_GROUP)


def _layer(x, w_in, b_gate, lq1, lk1, lq2, lk2, subln_w, conv_w, conv_b, dtb_f, dtb_b, alog_f, alog_b,
           d_skip, ssm_norm_w, w_branch, w_out, ln1_g, ln1_b, w_router, w_gate_e, w_up_e, w_down_e,
           ln2_g, ln2_b, layer_idx):
    bsz, seq, d = x.shape
    m = bsz * seq
    d_qk = ATTN_HEADS * 2 * ATTN_HEAD_DIM
    d_v = ATTN_HEADS * ATTN_V_DIM
    d_ssm = SSM_HEADS * SSM_HEAD_DIM
    d_conv = d_ssm + 2 * SSM_GROUPS * SSM_STATE
    n_main = 2 * d_qk + d_v + d_ssm + d_conv
    n_dt = 2 * SSM_HEADS
    lambda_init = 0.8 - 0.6 * math.exp(-0.3 * layer_idx)
    cap = CAPACITY_FACTOR * seq // N_EXPERTS
    row = lambda v: v.reshape(1, -1)

    w_in_t = w_in.T
    w_gate_t = w_in_t[n_main + n_dt:].astype(BF16)
    xb, dt_raw = _xcast_dt(x.reshape(m, d), w_in_t, n_main // LANES)
    dt_raw = dt_raw.reshape(bsz, seq, LANES)

    tn_main = 768
    proj = _matmul_wt(xb, w_in_t, 0, n_main // tn_main, F32, 1024, tn_main, "in_proj")
    proj = proj.reshape(bsz, seq, n_main)

    lane_pad = lambda a, bvec: jnp.pad(jnp.concatenate([a, bvec]), (0, LANES - n_dt)).reshape(1, LANES)
    dt, cs = _dtprep(dt_raw, lane_pad(dtb_f, dtb_b), lane_pad(alog_f, alog_b))
    dtg, csg = _group_heads(dt), _group_heads(cs)
    n_chunks = seq // SSM_CHUNK
    cstg = jnp.transpose(csg.reshape(bsz, SSM_GROUPS, n_chunks, SSM_CHUNK, 2 * HEADS_PER_GROUP),
                         (0, 1, 2, 4, 3))

    y_attn = _attention(proj, row(lq1), row(lk1), row(lq2), row(lk2), row(subln_w), lambda_init, 256)
    y_ssm = _ssd(proj, dtg, csg, cstg, conv_w, row(conv_b), row(jnp.repeat(d_skip, SSM_HEAD_DIM)),
                 row(ssm_norm_w), 2 * d_qk + d_v, 2 * d_qk + d_v + d_ssm)

    u = _merge(xb, w_gate_t, row(b_gate), y_attn.reshape(m, d_v), y_ssm.reshape(m, d_ssm),
               w_branch, 512, 512)
    w_router_p = jnp.pad(w_router, ((0, 0), (0, LANES - N_EXPERTS))).astype(BF16)
    x1, logits = _outproj_ln(u, w_out.astype(BF16), x.reshape(m, d), row(ln1_g), row(ln1_b),
                             w_router_p, 512, 512)
    x1 = x1.reshape(bsz, seq, d)

    slot, slot_t, gslot = _routing(logits.reshape(bsz, seq, LANES), cap)
    xg = _gather(slot, x1, cap, 512)
    h = _gateup(xg, w_gate_e, w_up_e, 256)
    gslot_e = jnp.transpose(gslot, (1, 0, 2, 3)).reshape(N_EXPERTS, bsz * cap, 1)
    yg = _down(h, w_down_e, gslot_e, 1024)
    return _scatter_ln(slot_t, yg, x1, row(ln2_g), row(ln2_b), cap, 512, 512)


def kernel(x, w_in, b_gate, lambda_q1, lambda_k1, lambda_q2, lambda_k2, attn_subln_w, conv_w, conv_b,
           dt_bias_fwd, dt_bias_bwd, a_log_fwd, a_log_bwd, d_skip, ssm_norm_w, w_branch, w_out,
           ln1_g, ln1_b, w_router, w_gate_e, w_up_e, w_down_e, ln2_g, ln2_b):
    for l in range(w_in.shape[0]):
        x = _layer(x, w_in[l], b_gate[l], lambda_q1[l], lambda_k1[l], lambda_q2[l], lambda_k2[l],
                   attn_subln_w[l], conv_w[l], conv_b[l], dt_bias_fwd[l], dt_bias_bwd[l],
                   a_log_fwd[l], a_log_bwd[l], d_skip[l], ssm_norm_w[l], w_branch[l], w_out[l],
                   ln1_g[l], ln1_b[l], w_router[l], w_gate_e[l], w_up_e[l], w_down_e[l],
                   ln2_g[l], ln2_b[l], l)
    return x
```

```python
import functools
import math

import jax
import jax.numpy as jnp
from jax import lax
from jax.experimental import pallas as pl
from jax.experimental.pallas import tpu as pltpu

F32 = jnp.float32
BF16 = jnp.bfloat16

ATTN_HEADS = 16
ATTN_HEAD_DIM = 64
ATTN_V_DIM = 128
ROPE_THETA = 10000.0
LOG2_E = math.log2(math.e)
SSM_HEAD_DIM = 64
SSM_HEADS = 32
SSM_GROUPS = 8
SSM_STATE = 128
SSM_CHUNK = 128
CONV_WIDTH = 5
N_EXPERTS = 16
CAPACITY_FACTOR = 2
DEPTH = 1
ALPHA = (2.0 * DEPTH) ** 0.25

LANES = 128
SUBLANES = 8
VMEM_LIMIT = 56 * 1024 * 1024
VMEM_LIMIT_BIG_TILES = 63 * 1024 * 1024

HEADS_PER_GROUP = SSM_HEADS // SSM_GROUPS
GROUP_W = HEADS_PER_GROUP * SSM_HEAD_DIM
CONV_PAD_ROWS = SUBLANES


def _cparams(sem, vmem_limit=VMEM_LIMIT):
    return pltpu.CompilerParams(dimension_semantics=sem, vmem_limit_bytes=vmem_limit)


def _sigmoid(x):
    return 1.0 / (1.0 + jnp.exp(-x))


def _silu(x):
    return x * _sigmoid(x)


def _softplus(x):
    return jnp.maximum(x, 0.0) + jnp.log1p(jnp.exp(-jnp.abs(x)))


def _layer_norm(r, g, b):
    mu = jnp.mean(r, axis=-1, keepdims=True)
    d = r - mu
    var = jnp.mean(d * d, axis=-1, keepdims=True)
    return d * lax.rsqrt(var + 1e-5) * g + b


def _mm_wt_kernel(a_ref, w_ref, side_ref, o_ref, side_b_ref, wb_ref):
    @pl.when(pl.program_id(1) == 0)
    def _():
        wb_ref[...] = w_ref[...].astype(BF16)

    o_ref[...] = lax.dot_general(a_ref[...], wb_ref[...], (((1,), (1,)), ((), ())),
                                 preferred_element_type=F32).astype(o_ref.dtype)
    side_b_ref[...] = side_ref[...].astype(BF16)


def _matmul_wt(a, w_t, n_blk, side_row0, side_rows, out_dtype, tm, tn, name):
    m, k = a.shape
    n_i = m // tm
    assert side_rows % (n_blk * n_i) == 0
    rs = side_rows // (n_blk * n_i)
    assert rs % 16 == 0 and side_row0 % rs == 0
    return pl.pallas_call(
        _mm_wt_kernel,
        grid=(n_blk, n_i),
        in_specs=[pl.BlockSpec((tm, k), lambda j, i: (i, 0)),
                  pl.BlockSpec((tn, k), lambda j, i: (j, 0)),
                  pl.BlockSpec((rs, k), lambda j, i: (side_row0 // rs + j * n_i + i, 0))],
        out_specs=[pl.BlockSpec((tm, tn), lambda j, i: (i, j)),
                   pl.BlockSpec((rs, k), lambda j, i: (j * n_i + i, 0))],
        out_shape=[jax.ShapeDtypeStruct((m, n_blk * tn), out_dtype),
                   jax.ShapeDtypeStruct((side_rows, k), BF16)],
        scratch_shapes=[pltpu.VMEM((tn, k), BF16)],
        compiler_params=_cparams(("arbitrary", "arbitrary"), VMEM_LIMIT_BIG_TILES),
        name=name,
    )(a, w_t, w_t)


def _xcast_dt_kernel(x_ref, w_ref, xb_ref, dt_ref):
    xb = x_ref[...].astype(BF16)
    xb_ref[...] = xb
    dt_ref[...] = lax.dot_general(xb, w_ref[...].astype(BF16), (((1,), (1,)), ((), ())),
                                  preferred_element_type=F32)


def _xcast_dt(x, w_t, blk):
    m, k = x.shape
    tm = 512
    return pl.pallas_call(
        _xcast_dt_kernel,
        grid=(m // tm,),
        in_specs=[pl.BlockSpec((tm, k), lambda i: (i, 0)),
                  pl.BlockSpec((LANES, k), lambda i: (blk, 0))],
        out_specs=[pl.BlockSpec((tm, k), lambda i: (i, 0)),
                   pl.BlockSpec((tm, LANES), lambda i: (i, 0))],
        out_shape=[jax.ShapeDtypeStruct((m, k), BF16), jax.ShapeDtypeStruct((m, LANES), F32)],
        compiler_params=_cparams(("arbitrary",)),
        name="x_cast_dt_proj",
    )(x, w_t)


def _dtprep_kernel(raw_ref, bias_ref, alog_ref, dt_ref, cs_ref):
    seq = raw_ref.shape[1]
    lc = SSM_CHUNK
    dt = _softplus(raw_ref[0] + bias_ref[...])
    dt_ref[0] = dt
    la = dt * (-jnp.exp(alog_ref[...]))
    row = lax.broadcasted_iota(jnp.int32, (lc, lc), 0)
    col = lax.broadcasted_iota(jnp.int32, (lc, lc), 1)
    t_low = jnp.where(row >= col, 1.0, 0.0).astype(F32)
    t_up = jnp.where(row <= col, 1.0, 0.0).astype(F32)
    fwd_lane = lax.broadcasted_iota(jnp.int32, (1, LANES), 1) < SSM_HEADS
    for c in range(seq // lc):
        lac = la[c * lc:(c + 1) * lc]
        f = jnp.dot(t_low, lac, preferred_element_type=F32, precision=lax.Precision.HIGHEST)
        b = jnp.dot(t_up, lac, preferred_element_type=F32, precision=lax.Precision.HIGHEST)
        cs_ref[0, c * lc:(c + 1) * lc, :] = jnp.where(fwd_lane, f, b)


def _dtprep(raw, bias, alog):
    bsz, seq, _ = raw.shape
    blk = pl.BlockSpec((1, seq, LANES), lambda b: (b, 0, 0))
    par = pl.BlockSpec((1, LANES), lambda b: (0, 0))
    return pl.pallas_call(
        _dtprep_kernel,
        grid=(bsz,),
        in_specs=[blk, par, par],
        out_specs=[blk, blk],
        out_shape=[jax.ShapeDtypeStruct(raw.shape, F32)] * 2,
        compiler_params=_cparams(("arbitrary",)),
        name="dt_prep",
    )(raw, bias, alog)


def _attn_kernel(lq1_ref, lk1_ref, lq2_ref, lk2_ref, sw_ref, cos_ref, sin_ref,
                 q_ref, k_ref, v_ref, o_ref, kr_ref, vb_ref, qs_ref, sa_ref, sb_ref, *, lambda_init, tq):
    seq = q_ref.shape[1]
    n_tiles = seq // tq
    lane = lax.broadcasted_iota(jnp.int32, (1, LANES), 1)
    first_half = (lane & (ATTN_HEAD_DIM // 2)) == 0
    comp1 = lane < ATTN_HEAD_DIM

    def rope(x, c, s):
        partner = jnp.where(first_half,
                            pltpu.roll(x, LANES - ATTN_HEAD_DIM // 2, 1),
                            pltpu.roll(x, ATTN_HEAD_DIM // 2, 1))
        return x * c + partner * s

    kr_ref[...] = rope(k_ref[0], cos_ref[...], sin_ref[...]).astype(BF16)
    vb_ref[:, 0:ATTN_V_DIM] = v_ref[0].astype(BF16)
    vb_ref[:, ATTN_V_DIM:] = jnp.where(lane == 0, 1.0, 0.0).astype(BF16) * jnp.ones((seq, 1), BF16)
    q = rope(q_ref[0], cos_ref[...], sin_ref[...]) * (ATTN_HEAD_DIM ** -0.5 * LOG2_E)
    for t in range(n_tiles):
        qt = q[t * tq:(t + 1) * tq]
        qs_ref[t, 0:tq, :] = jnp.where(comp1, qt, 0.0).astype(BF16)
        qs_ref[t, tq:2 * tq, :] = jnp.where(comp1, 0.0, qt).astype(BF16)

    lam = (jnp.exp(jnp.sum(lq1_ref[...] * lk1_ref[...], axis=-1, keepdims=True))
           - jnp.exp(jnp.sum(lq2_ref[...] * lk2_ref[...], axis=-1, keepdims=True)) + lambda_init)

    def scores(t, dst_ref):
        dst_ref[...] = lax.dot_general(qs_ref[t], kr_ref[...], (((1,), (1,)), ((), ())),
                                       preferred_element_type=F32)

    def finish(t, src_ref):
        s = src_ref[...]
        p = jnp.exp2(s - jnp.max(s, axis=-1, keepdims=True)).astype(BF16)
        pv = jnp.dot(p, vb_ref[...], preferred_element_type=F32)
        inv = 1.0 / pv[:, ATTN_V_DIM:ATTN_V_DIM + 1]
        o = pv[:tq, :ATTN_V_DIM] * inv[:tq] - pv[tq:, :ATTN_V_DIM] * (lam * inv[tq:])
        o = o * lax.rsqrt(jnp.mean(o * o, axis=-1, keepdims=True) + 1e-6) * sw_ref[...]
        o_ref[0, pl.ds(pl.multiple_of(t * tq, tq), tq), :] = (o * (1.0 - lambda_init)).astype(o_ref.dtype)

    scores(0, sa_ref)

    def pair(j, carry):
        t = 2 * j
        scores(t + 1, sb_ref)
        finish(t, sa_ref)
        scores(t + 2, sa_ref)
        finish(t + 1, sb_ref)
        return carry

    for j in range(n_tiles // 2 - 1):
        pair(j, 0)
    scores(n_tiles - 1, sb_ref)
    finish(n_tiles - 2, sa_ref)
    finish(n_tiles - 1, sb_ref)


def _rope_tables(seq):
    half = ATTN_HEAD_DIM // 2
    inv_freq = ROPE_THETA ** (-jnp.arange(0, ATTN_HEAD_DIM, 2, dtype=F32) / ATTN_HEAD_DIM)
    ang = jnp.arange(seq, dtype=F32)[:, None] * inv_freq[None, :]
    cos, sin = jnp.cos(ang), jnp.sin(ang)
    reps = LANES // half
    sign = jnp.tile(jnp.concatenate([-jnp.ones((half,), F32), jnp.ones((half,), F32)]), reps // 2)
    return jnp.tile(cos, (1, reps)), jnp.tile(sin, (1, reps)) * sign[None, :]


def _attention(proj, lq1, lk1, lq2, lk2, subln_w, lambda_init, tq):
    bsz, seq, _ = proj.shape
    cos, sin = _rope_tables(seq)
    assert seq % (2 * tq) == 0 and seq // tq >= 2
    vec = pl.BlockSpec((1, ATTN_HEAD_DIM), lambda b, h: (0, 0))
    table = pl.BlockSpec((seq, LANES), lambda b, h: (0, 0))
    return pl.pallas_call(
        functools.partial(_attn_kernel, lambda_init=lambda_init, tq=tq),
        grid=(bsz, ATTN_HEADS),
        in_specs=[vec, vec, vec, vec,
                  pl.BlockSpec((1, LANES), lambda b, h: (0, 0)),
                  table, table,
                  pl.BlockSpec((1, seq, LANES), lambda b, h: (b, 0, h)),
                  pl.BlockSpec((1, seq, LANES), lambda b, h: (b, 0, ATTN_HEADS + h)),
                  pl.BlockSpec((1, seq, LANES), lambda b, h: (b, 0, 2 * ATTN_HEADS + h))],
        out_specs=pl.BlockSpec((1, seq, LANES), lambda b, h: (b, 0, h)),
        out_shape=jax.ShapeDtypeStruct((bsz, seq, ATTN_HEADS * ATTN_V_DIM), BF16),
        scratch_shapes=[pltpu.VMEM((seq, LANES), BF16), pltpu.VMEM((seq, ATTN_V_DIM + LANES), BF16),
                        pltpu.VMEM((seq // tq, 2 * tq, LANES), BF16),
                        pltpu.VMEM((2 * tq, seq), F32), pltpu.VMEM((2 * tq, seq), F32)],
        compiler_params=_cparams(("arbitrary", "arbitrary")),
        name="diff_attention",
    )(lq1, lk1, lq2, lk2, subln_w, cos, sin, proj, proj, proj)


def _ssd_kernel(z_ref, x_ref, b_ref, c_ref, dt_ref, cs_ref, cst_ref,
                cwx_ref, cwb_ref, cwc_ref, cbx_ref, cbb_ref, cbc_ref, dsk_ref, nw_ref,
                o_ref, pad_ref, xs_ref, bm_ref, cm_ref, y_ref, h_ref):
    seq = x_ref.shape[1]
    lc = SSM_CHUNK
    n_chunks = seq // lc
    row_tile = 256

    def conv_silu(in_ref, w_ref, bias_ref, out_ref, width):
        zeros = jnp.zeros((CONV_PAD_ROWS, width), F32)
        pad_ref[0:CONV_PAD_ROWS, 0:width] = zeros
        pad_ref[CONV_PAD_ROWS + seq:2 * CONV_PAD_ROWS + seq, 0:width] = zeros
        pad_ref[CONV_PAD_ROWS:CONV_PAD_ROWS + seq, 0:width] = in_ref[0]
        half = (CONV_WIDTH - 1) // 2
        for t in range(seq // row_tile):
            acc = jnp.broadcast_to(bias_ref[...], (row_tile, width))
            for j in range(CONV_WIDTH):
                start = CONV_PAD_ROWS + t * row_tile + j - half
                acc = acc + pad_ref[start:start + row_tile, 0:width] * w_ref[j:j + 1, :]
            out_ref[t * row_tile:(t + 1) * row_tile, :] = _silu(acc)

    conv_silu(x_ref, cwx_ref, cbx_ref, xs_ref, GROUP_W)
    conv_silu(b_ref, cwb_ref, cbb_ref, bm_ref, SSM_STATE)
    conv_silu(c_ref, cwc_ref, cbc_ref, cm_ref, SSM_STATE)

    head_of_lane = lax.broadcasted_iota(jnp.int32, (1, GROUP_W), 1) // SSM_HEAD_DIM
    row = lax.broadcasted_iota(jnp.int32, (lc, lc), 0)
    col = lax.broadcasted_iota(jnp.int32, (lc, lc), 1)

    def expand(cols, off):
        out = cols[:, off + HEADS_PER_GROUP - 1:off + HEADS_PER_GROUP]
        for r in range(HEADS_PER_GROUP - 2, -1, -1):
            out = jnp.where(head_of_lane == r, cols[:, off + r:off + r + 1], out)
        return out

    def run_direction(reverse):
        off = HEADS_PER_GROUP if reverse else 0
        mask = (row <= col) if reverse else (row >= col)
        edge = 0 if reverse else lc - 1
        h_ref[...] = jnp.zeros_like(h_ref)

        def body(ci, carry):
            c = (n_chunks - 1 - ci) if reverse else ci
            r0 = pl.multiple_of(c * lc, lc)
            xc = xs_ref[pl.ds(r0, lc), :]
            bc = bm_ref[pl.ds(r0, lc), :]
            cc = cm_ref[pl.ds(r0, lc), :].astype(BF16)
            dtc = dt_ref[0, 0, pl.ds(r0, lc), :]
            csc = cs_ref[0, 0, pl.ds(r0, lc), :]
            cst = cst_ref[0, 0, c]
            cs_e = expand(csc, off)
            edge_e = expand(csc[edge:edge + 1, :], off)
            xdt = xc * expand(dtc, off)
            xdt_b = xdt.astype(BF16)
            cb = lax.dot_general(cc, bc.astype(BF16), (((1,), (1,)), ((), ())),
                                 preferred_element_type=F32)
            y = jnp.zeros((lc, GROUP_W), F32)
            for r in range(HEADS_PER_GROUP):
                diff = csc[:, off + r:off + r + 1] - cst[off + r:off + r + 1, :]
                decay = jnp.exp(jnp.where(mask, diff, -jnp.inf))
                yr = jnp.dot((cb * decay).astype(BF16), xdt_b, preferred_element_type=F32)
                y = jnp.where(head_of_lane == r, yr, y)
            h_t = h_ref[...]
            y = y + jnp.dot(cc, h_t.astype(BF16), preferred_element_type=F32) * jnp.exp(cs_e)
            new_state = jnp.dot(bc.T.astype(BF16), (xdt * jnp.exp(edge_e - cs_e)).astype(BF16),
                                preferred_element_type=F32)
            h_ref[...] = h_t * jnp.exp(edge_e) + new_state
            if reverse:
                y_ref[pl.ds(r0, lc), :] += y
            else:
                y_ref[pl.ds(r0, lc), :] = y
            return carry

        lax.fori_loop(0, n_chunks, body, 0, unroll=8)

    run_direction(False)
    run_direction(True)

    for t in range(seq // row_tile):
        rows = slice(t * row_tile, (t + 1) * row_tile)
        y = y_ref[rows, :] + xs_ref[rows, :] * dsk_ref[...]
        y = y * _silu(z_ref[0, rows, :])
        y = y * lax.rsqrt(jnp.mean(y * y, axis=-1, keepdims=True) + 1e-6) * nw_ref[...]
        o_ref[0, rows, :] = y.astype(o_ref.dtype)


def _ssd(proj, dtg, csg, cstg, conv_w, conv_b, dskip_e, norm_w, col0_z, col0_xbc):
    bsz, seq, _ = proj.shape
    d_ssm = SSM_HEADS * SSM_HEAD_DIM
    gn = SSM_GROUPS * SSM_STATE
    n_chunks = seq // SSM_CHUNK
    zb, xb = col0_z // GROUP_W, col0_xbc // GROUP_W
    bb, cb = (col0_xbc + d_ssm) // SSM_STATE, (col0_xbc + d_ssm + gn) // SSM_STATE
    wb, wc = d_ssm // SSM_STATE, (d_ssm + gn) // SSM_STATE
    n_dir_heads = 2 * HEADS_PER_GROUP
    return pl.pallas_call(
        _ssd_kernel,
        grid=(bsz, SSM_GROUPS),
        in_specs=[pl.BlockSpec((1, seq, GROUP_W), lambda b, g: (b, 0, zb + g)),
                  pl.BlockSpec((1, seq, GROUP_W), lambda b, g: (b, 0, xb + g)),
                  pl.BlockSpec((1, seq, SSM_STATE), lambda b, g: (b, 0, bb + g)),
                  pl.BlockSpec((1, seq, SSM_STATE), lambda b, g: (b, 0, cb + g)),
                  pl.BlockSpec((1, 1, seq, n_dir_heads), lambda b, g: (b, g, 0, 0)),
                  pl.BlockSpec((1, 1, seq, n_dir_heads), lambda b, g: (b, g, 0, 0)),
                  pl.BlockSpec((1, 1, n_chunks, n_dir_heads, SSM_CHUNK), lambda b, g: (b, g, 0, 0, 0)),
                  pl.BlockSpec((CONV_WIDTH, GROUP_W), lambda b, g: (0, g)),
                  pl.BlockSpec((CONV_WIDTH, SSM_STATE), lambda b, g: (0, wb + g)),
                  pl.BlockSpec((CONV_WIDTH, SSM_STATE), lambda b, g: (0, wc + g)),
                  pl.BlockSpec((1, GROUP_W), lambda b, g: (0, g)),
                  pl.BlockSpec((1, SSM_STATE), lambda b, g: (0, wb + g)),
                  pl.BlockSpec((1, SSM_STATE), lambda b, g: (0, wc + g)),
                  pl.BlockSpec((1, GROUP_W), lambda b, g: (0, g)),
                  pl.BlockSpec((1, GROUP_W), lambda b, g: (0, g))],
        out_specs=pl.BlockSpec((1, seq, GROUP_W), lambda b, g: (b, 0, g)),
        out_shape=jax.ShapeDtypeStruct((bsz, seq, d_ssm), BF16),
        scratch_shapes=[pltpu.VMEM((seq + 2 * CONV_PAD_ROWS, GROUP_W), F32),
                        pltpu.VMEM((seq, GROUP_W), F32),
                        pltpu.VMEM((seq, SSM_STATE), F32),
                        pltpu.VMEM((seq, SSM_STATE), F32),
                        pltpu.VMEM((seq, GROUP_W), F32),
                        pltpu.VMEM((SSM_STATE, GROUP_W), F32)],
        compiler_params=_cparams(("arbitrary", "arbitrary")),
        name="ssd",
    )(proj, proj, proj, proj, dtg, csg, cstg, conv_w, conv_w, conv_w, conv_b, conv_b, conv_b,
      dskip_e, norm_w)


def _merge_kernel(x_ref, wga_ref, wgs_ref, bga_ref, bgs_ref, ya_ref, ys_ref, wba_ref, wbs_ref, u_ref,
                  wba_b_ref, wbs_b_ref):
    @pl.when(pl.program_id(1) == 0)
    def _():
        wba_b_ref[...] = wba_ref[...].astype(BF16)
        wbs_b_ref[...] = wbs_ref[...].astype(BF16)

    x = x_ref[...]
    nt = (((1,), (1,)), ((), ()))
    ga = _sigmoid(lax.dot_general(x, wga_ref[...], nt, preferred_element_type=F32) + bga_ref[...])
    gs = _sigmoid(lax.dot_general(x, wgs_ref[...], nt, preferred_element_type=F32) + bgs_ref[...])
    ba = jnp.dot(ya_ref[...], wba_b_ref[...], preferred_element_type=F32)
    bs = jnp.dot(ys_ref[...], wbs_b_ref[...], preferred_element_type=F32)
    u_ref[...] = (ga * ba + gs * bs).astype(u_ref.dtype)


def _merge(xb, w_gate, b_gate, y_attn, y_ssm, w_branch, tm, tn):
    m, d = xb.shape
    n = w_branch.shape[1]
    ka, ks = y_attn.shape[1], y_ssm.shape[1]
    nj = n // tn
    return pl.pallas_call(
        _merge_kernel,
        grid=(nj, m // tm),
        in_specs=[pl.BlockSpec((tm, d), lambda j, i: (i, 0)),
                  pl.BlockSpec((tn, d), lambda j, i: (j, 0)),
                  pl.BlockSpec((tn, d), lambda j, i: (nj + j, 0)),
                  pl.BlockSpec((1, tn), lambda j, i: (0, j)),
                  pl.BlockSpec((1, tn), lambda j, i: (0, nj + j)),
                  pl.BlockSpec((tm, ka), lambda j, i: (i, 0)),
                  pl.BlockSpec((tm, ks), lambda j, i: (i, 0)),
                  pl.BlockSpec((ka, tn), lambda j, i: (0, j)),
                  pl.BlockSpec((ks, tn), lambda j, i: (ka // ks, j))],
        out_specs=pl.BlockSpec((tm, tn), lambda j, i: (i, j)),
        out_shape=jax.ShapeDtypeStruct((m, n), BF16),
        scratch_shapes=[pltpu.VMEM((ka, tn), BF16), pltpu.VMEM((ks, tn), BF16)],
        compiler_params=_cparams(("arbitrary", "arbitrary"), VMEM_LIMIT_BIG_TILES),
        name="gated_merge",
    )(xb, w_gate, w_gate, b_gate, b_gate, y_attn, y_ssm, w_branch, w_branch)


def _outproj_kernel(u_ref, w_ref, x_ref, g_ref, b_ref, wr_ref, x1_ref, lg_ref):
    j = pl.program_id(1)
    tn = w_ref.shape[1]
    n_slabs = x1_ref.shape[1] // tn
    part = jnp.dot(u_ref[...], w_ref[...], preferred_element_type=F32)

    for slab in range(n_slabs):
        @pl.when(j == slab)
        def _(slab=slab):
            x1_ref[:, slab * tn:(slab + 1) * tn] = part

    @pl.when(j == n_slabs - 1)
    def _():
        x1 = _layer_norm(ALPHA * x_ref[...] + x1_ref[...], g_ref[...], b_ref[...])
        x1_ref[...] = x1
        lg_ref[...] = jnp.dot(x1.astype(BF16), wr_ref[...], preferred_element_type=F32)


def _outproj_ln(u, w_out, x, g, b, w_router_p, tm, tn):
    m, kdim = u.shape
    d = w_out.shape[1]
    row = pl.BlockSpec((tm, d), lambda i, j: (i, 0))
    par = pl.BlockSpec((1, d), lambda i, j: (0, 0))
    return pl.pallas_call(
        _outproj_kernel,
        grid=(m // tm, d // tn),
        in_specs=[pl.BlockSpec((tm, kdim), lambda i, j: (i, 0)),
                  pl.BlockSpec((kdim, tn), lambda i, j: (0, j)),
                  pl.BlockSpec((tm, d), lambda i, j: (i, 0)),
                  par, par,
                  pl.BlockSpec((d, LANES), lambda i, j: (0, 0), pipeline_mode=pl.Buffered(1))],
        out_specs=[row, pl.BlockSpec((tm, LANES), lambda i, j: (i, 0))],
        out_shape=[jax.ShapeDtypeStruct((m, d), F32), jax.ShapeDtypeStruct((m, LANES), F32)],
        compiler_params=_cparams(("arbitrary", "arbitrary"), VMEM_LIMIT_BIG_TILES),
        name="out_proj_ln1",
    )(u, w_out, x, g, b, w_router_p)


def _routing_kernel(lg_ref, slot_ref, slott_ref, gslot_ref, aff_ref, *, cap):
    seq = lg_ref.shape[1]
    blk = 256
    lg = lg_ref[0]
    valid = lax.broadcasted_iota(jnp.int32, lg.shape, 1) < N_EXPERTS
    lgm = jnp.where(valid, lg, -jnp.inf)
    ex = jnp.exp(lgm - jnp.max(lgm, axis=-1, keepdims=True))
    aff = ex / jnp.sum(ex, axis=-1, keepdims=True)
    aff_t = aff.T[0:N_EXPERTS]
    aff_ref[...] = aff_t
    bits = lax.bitcast_convert_type(aff_t, jnp.int32)

    def count(m):
        return jnp.sum(jnp.where(m, 1.0, 0.0), axis=-1, keepdims=True)

    def search(i, thr):
        cand = thr | jnp.left_shift(jnp.int32(1), 30 - i)
        return jnp.where(count(bits >= cand) >= cap, cand, thr)

    thr = lax.fori_loop(0, 31, search, jnp.zeros((N_EXPERTS, 1), jnp.int32))
    gt = bits > thr
    eq = bits == thr

    r_i = lax.broadcasted_iota(jnp.int32, (blk, blk), 0)
    c_i = lax.broadcasted_iota(jnp.int32, (blk, blk), 1)
    before = jnp.where(r_i < c_i, 1.0, 0.0).astype(BF16)

    def excl_cumsum(m):
        mf = jnp.where(m, 1.0, 0.0)
        carry = jnp.zeros((N_EXPERTS, 1), F32)
        parts = []
        for k in range(seq // blk):
            piece = mf[:, k * blk:(k + 1) * blk]
            parts.append(jnp.dot(piece.astype(BF16), before, preferred_element_type=F32) + carry)
            carry = carry + jnp.sum(piece, axis=-1, keepdims=True)
        return jnp.concatenate(parts, axis=-1)

    need = cap - count(gt)
    sel = gt | (eq & (excl_cumsum(eq) < need))
    slot = jnp.where(sel, excl_cumsum(sel), -1.0)
    slot_ref[0] = slot
    pad = jnp.full((LANES - N_EXPERTS, seq), -1.0, F32)
    slott_ref[0] = jnp.concatenate([slot, pad], axis=0).T

    j_iota = lax.broadcasted_iota(jnp.int32, (cap, seq), 0).astype(F32)

    def gate_of_slot(e, carry):
        hit = slot_ref[0, pl.ds(e, 1), :] == j_iota
        gslot_ref[0, e] = jnp.sum(jnp.where(hit, aff_ref[pl.ds(e, 1), :], 0.0), axis=-1, keepdims=True)
        return carry

    lax.fori_loop(0, N_EXPERTS, gate_of_slot, 0)


def _routing(logits, cap):
    bsz, seq, _ = logits.shape
    return pl.pallas_call(
        functools.partial(_routing_kernel, cap=cap),
        grid=(bsz,),
        in_specs=[pl.BlockSpec((1, seq, LANES), lambda b: (b, 0, 0))],
        out_specs=[pl.BlockSpec((1, N_EXPERTS, seq), lambda b: (b, 0, 0)),
                   pl.BlockSpec((1, seq, LANES), lambda b: (b, 0, 0)),
                   pl.BlockSpec((1, N_EXPERTS, cap, 1), lambda b: (b, 0, 0, 0))],
        out_shape=[jax.ShapeDtypeStruct((bsz, N_EXPERTS, seq), F32),
                   jax.ShapeDtypeStruct((bsz, seq, LANES), F32),
                   jax.ShapeDtypeStruct((bsz, N_EXPERTS, cap, 1), F32)],
        scratch_shapes=[pltpu.VMEM((N_EXPERTS, seq), F32)],
        compiler_params=_cparams(("arbitrary",)),
        name="routing",
    )(logits)


def _gather_kernel(slot_ref, x_ref, o_ref, pick_ref, *, cap):
    seq = x_ref.shape[1]
    td = x_ref.shape[2]

    @pl.when(pl.program_id(1) == 0)
    def _():
        j_iota = lax.broadcasted_iota(jnp.int32, (cap, seq), 0).astype(F32)

        def one_expert(e, carry):
            hit = slot_ref[0, pl.ds(e, 1), :] == j_iota
            pick_ref[pl.ds(pl.multiple_of(e * cap, cap), cap), :] = jnp.where(hit, 1.0, 0.0).astype(BF16)
            return carry

        lax.fori_loop(0, N_EXPERTS, one_expert, 0)

    rows = jnp.dot(pick_ref[...], x_ref[0].astype(BF16), preferred_element_type=F32)
    o_ref[...] = rows.reshape(N_EXPERTS, cap, td).astype(o_ref.dtype)


def _gather(slot, x1, cap, td):
    bsz, seq, d = x1.shape
    return pl.pallas_call(
        functools.partial(_gather_kernel, cap=cap),
        grid=(bsz, d // td),
        in_specs=[pl.BlockSpec((1, N_EXPERTS, seq), lambda b, j: (b, 0, 0)),
                  pl.BlockSpec((1, seq, td), lambda b, j: (b, 0, j))],
        out_specs=pl.BlockSpec((N_EXPERTS, cap, td), lambda b, j: (0, b, j)),
        out_shape=jax.ShapeDtypeStruct((N_EXPERTS, bsz * cap, d), BF16),
        scratch_shapes=[pltpu.VMEM((N_EXPERTS * cap, seq), BF16)],
        compiler_params=_cparams(("arbitrary", "arbitrary")),
        name="moe_gather",
    )(slot, x1)


def _gateup_kernel(xg_ref, wg_ref, wu_ref, h_ref):
    xg = xg_ref[0]
    g = jnp.dot(xg, wg_ref[0].astype(BF16), preferred_element_type=F32)
    u = jnp.dot(xg, wu_ref[0].astype(BF16), preferred_element_type=F32)
    h_ref[0] = (_silu(g) * u).astype(h_ref.dtype)


def _gateup(xg, w_gate_e, w_up_e, tf):
    n_e, rows, d = xg.shape
    ff = w_gate_e.shape[2]
    wspec = pl.BlockSpec((1, d, tf), lambda e, f: (e, 0, f))
    return pl.pallas_call(
        _gateup_kernel,
        grid=(n_e, ff // tf),
        in_specs=[pl.BlockSpec((1, rows, d), lambda e, f: (e, 0, 0)), wspec, wspec],
        out_specs=pl.BlockSpec((1, rows, tf), lambda e, f: (e, 0, f)),
        out_shape=jax.ShapeDtypeStruct((n_e, rows, ff), BF16),
        compiler_params=_cparams(("arbitrary", "arbitrary")),
        name="moe_gate_up",
    )(xg, w_gate_e, w_up_e)


def _down_kernel(h_ref, wd_ref, gs_ref, y_ref):
    y = jnp.dot(h_ref[0], wd_ref[0].astype(BF16), preferred_element_type=F32)
    y_ref[0] = (y * gs_ref[0]).astype(y_ref.dtype)


def _down(h, w_down_e, gslot, td):
    n_e, rows, ff = h.shape
    d = w_down_e.shape[2]
    return pl.pallas_call(
        _down_kernel,
        grid=(n_e, d // td),
        in_specs=[pl.BlockSpec((1, rows, ff), lambda e, j: (e, 0, 0)),
                  pl.BlockSpec((1, ff, td), lambda e, j: (e, 0, j)),
                  pl.BlockSpec((1, rows, 1), lambda e, j: (e, 0, 0))],
        out_specs=pl.BlockSpec((1, rows, td), lambda e, j: (e, 0, j)),
        out_shape=jax.ShapeDtypeStruct((n_e, rows, d), BF16),
        compiler_params=_cparams(("arbitrary", "arbitrary")),
        name="moe_down",
    )(h, w_down_e, gslot)


def _scatter_kernel(slott_ref, yg_ref, x1_ref, g_ref, b_ref, o_ref, put_ref, *, cap):
    dj = pl.program_id(2)
    ts = o_ref.shape[1]
    td = yg_ref.shape[2]
    n_slabs = o_ref.shape[2] // td

    @pl.when(dj == 0)
    def _():
        st = slott_ref[0]
        lane = lax.broadcasted_iota(jnp.int32, st.shape, 1)
        j_iota = lax.broadcasted_iota(jnp.int32, (ts, cap), 1).astype(F32)
        for e in range(N_EXPERTS):
            col = jnp.sum(jnp.where(lane == e, st, 0.0), axis=-1, keepdims=True)
            put_ref[:, e * cap:(e + 1) * cap] = jnp.where(col == j_iota, 1.0, 0.0).astype(BF16)

    part = jnp.dot(put_ref[...], yg_ref[...].reshape(N_EXPERTS * cap, td), preferred_element_type=F32)

    for slab in range(n_slabs):
        @pl.when(dj == slab)
        def _(slab=slab):
            o_ref[0, :, slab * td:(slab + 1) * td] = part

    @pl.when(dj == n_slabs - 1)
    def _():
        o_ref[0] = _layer_norm(ALPHA * x1_ref[0] + o_ref[0], g_ref[...], b_ref[...])


def _scatter_ln(slot_t, yg, x1, g, b, cap, ts, td):
    bsz, seq, d = x1.shape
    par = pl.BlockSpec((1, d), lambda bi, i, j: (0, 0))
    return pl.pallas_call(
        functools.partial(_scatter_kernel, cap=cap),
        grid=(bsz, seq // ts, d // td),
        in_specs=[pl.BlockSpec((1, ts, LANES), lambda bi, i, j: (bi, i, 0)),
                  pl.BlockSpec((N_EXPERTS, cap, td), lambda bi, i, j: (0, bi, j)),
                  pl.BlockSpec((1, ts, d), lambda bi, i, j: (bi, i, 0)),
                  par, par],
        out_specs=pl.BlockSpec((1, ts, d), lambda bi, i, j: (bi, i, 0)),
        out_shape=jax.ShapeDtypeStruct((bsz, seq, d), F32),
        scratch_shapes=[pltpu.VMEM((ts, N_EXPERTS * cap), BF16)],
        compiler_params=_cparams(("arbitrary", "arbitrary", "arbitrary"), VMEM_LIMIT_BIG_TILES),
        name="moe_scatter_ln2",
    )(slot_t, yg, x1, g, b)


def _group_heads(t):
    bsz, seq, _ = t.shape
    t = t[:, :, :2 * SSM_HEADS].reshape(bsz, seq, 2, SSM_GROUPS, HEADS_PER_GROUP)
    return jnp.transpose(t, (0, 3, 1, 2, 4)).reshape(bsz, SSM_GROUPS, seq, 2 * HEADS_PER_GROUP)


def _layer(x, w_in, b_gate, lq1, lk1, lq2, lk2, subln_w, conv_w, conv_b, dtb_f, dtb_b, alog_f, alog_b,
           d_skip, ssm_norm_w, w_branch, w_out, ln1_g, ln1_b, w_router, w_gate_e, w_up_e, w_down_e,
           ln2_g, ln2_b, layer_idx):
    bsz, seq, d = x.shape
    m = bsz * seq
    d_qk = ATTN_HEADS * 2 * ATTN_HEAD_DIM
    d_v = ATTN_HEADS * ATTN_V_DIM
    d_ssm = SSM_HEADS * SSM_HEAD_DIM
    d_conv = d_ssm + 2 * SSM_GROUPS * SSM_STATE
    n_main = 2 * d_qk + d_v + d_ssm + d_conv
    n_dt = 2 * SSM_HEADS
    lambda_init = 0.8 - 0.6 * math.exp(-0.3 * layer_idx)
    cap = CAPACITY_FACTOR * seq // N_EXPERTS
    row = lambda v: v.reshape(1, -1)

    w_in_t = w_in.T
    xb, dt_raw = _xcast_dt(x.reshape(m, d), w_in_t, n_main // LANES)
    dt_raw = dt_raw.reshape(bsz, seq, LANES)

    tn_main = 768
    proj, w_gate_t = _matmul_wt(xb, w_in_t, n_main // tn_main, n_main + n_dt, 2 * d, F32, 1024, tn_main,
                                "in_proj")
    proj = proj.reshape(bsz, seq, n_main)

    lane_pad = lambda a, bvec: jnp.pad(jnp.concatenate([a, bvec]), (0, LANES - n_dt)).reshape(1, LANES)
    dt, cs = _dtprep(dt_raw, lane_pad(dtb_f, dtb_b), lane_pad(alog_f, alog_b))
    dtg, csg = _group_heads(dt), _group_heads(cs)
    n_chunks = seq // SSM_CHUNK
    cstg = jnp.transpose(csg.reshape(bsz, SSM_GROUPS, n_chunks, SSM_CHUNK, 2 * HEADS_PER_GROUP),
                         (0, 1, 2, 4, 3))

    y_attn = _attention(proj, row(lq1), row(lk1), row(lq2), row(lk2), row(subln_w), lambda_init, 256)
    y_ssm = _ssd(proj, dtg, csg, cstg, conv_w, row(conv_b), row(jnp.repeat(d_skip, SSM_HEAD_DIM)),
                 row(ssm_norm_w), 2 * d_qk + d_v, 2 * d_qk + d_v + d_ssm)

    u = _merge(xb, w_gate_t, row(b_gate), y_attn.reshape(m, d_v), y_ssm.reshape(m, d_ssm),
               w_branch, 512, 512)
    w_router_p = jnp.pad(w_router, ((0, 0), (0, LANES - N_EXPERTS))).astype(BF16)
    x1, logits = _outproj_ln(u, w_out.astype(BF16), x.reshape(m, d), row(ln1_g), row(ln1_b),
                             w_router_p, 512, 512)
    x1 = x1.reshape(bsz, seq, d)

    slot, slot_t, gslot = _routing(logits.reshape(bsz, seq, LANES), cap)
    xg = _gather(slot, x1, cap, 512)
    h = _gateup(xg, w_gate_e, w_up_e, 256)
    gslot_e = jnp.transpose(gslot, (1, 0, 2, 3)).reshape(N_EXPERTS, bsz * cap, 1)
    yg = _down(h, w_down_e, gslot_e, 1024)
    return _scatter_ln(slot_t, yg, x1, row(ln2_g), row(ln2_b), cap, 512, 512)


def kernel(x, w_in, b_gate, lambda_q1, lambda_k1, lambda_q2, lambda_k2, attn_subln_w, conv_w, conv_b,
           dt_bias_fwd, dt_bias_bwd, a_log_fwd, a_log_bwd, d_skip, ssm_norm_w, w_branch, w_out,
           ln1_g, ln1_b, w_router, w_gate_e, w_up_e, w_down_e, ln2_g, ln2_b):
    for l in range(w_in.shape[0]):
        x = _layer(x, w_in[l], b_gate[l], lambda_q1[l], lambda_k1[l], lambda_q2[l], lambda_k2[l],
                   attn_subln_w[l], conv_w[l], conv_b[l], dt_bias_fwd[l], dt_bias_bwd[l],
                   a_log_fwd[l], a_log_bwd[l], d_skip[l], ssm_norm_w[l], w_branch[l], w_out[l],
                   ln1_g[l], ln1_b[l], w_router[l], w_gate_e[l], w_up_e[l], w_down_e[l],
                   ln2_g[l], ln2_b[l], l)
    return x
```

```python
import functools
import math

import jax
import jax.numpy as jnp
from jax import lax
from jax.experimental import pallas as pl
from jax.experimental.pallas import tpu as pltpu

F32 = jnp.float32
BF16 = jnp.bfloat16

ATTN_HEADS = 16
ATTN_HEAD_DIM = 64
ATTN_V_DIM = 128
ROPE_THETA = 10000.0
LOG2_E = math.log2(math.e)
SSM_HEAD_DIM = 64
SSM_HEADS = 32
SSM_GROUPS = 8
SSM_STATE = 128
SSM_CHUNK = 128
CONV_WIDTH = 5
N_EXPERTS = 16
CAPACITY_FACTOR = 2
DEPTH = 1
ALPHA = (2.0 * DEPTH) ** 0.25

LANES = 128
SUBLANES = 8
VMEM_LIMIT = 56 * 1024 * 1024
VMEM_LIMIT_BIG_TILES = 63 * 1024 * 1024

HEADS_PER_GROUP = SSM_HEADS // SSM_GROUPS
GROUP_W = HEADS_PER_GROUP * SSM_HEAD_DIM
CONV_PAD_ROWS = SUBLANES


def _cparams(sem, vmem_limit=VMEM_LIMIT):
    return pltpu.CompilerParams(dimension_semantics=sem, vmem_limit_bytes=vmem_limit)


def _sigmoid(x):
    return 1.0 / (1.0 + jnp.exp(-x))


def _silu(x):
    return x * _sigmoid(x)


def _softplus(x):
    return jnp.maximum(x, 0.0) + jnp.log1p(jnp.exp(-jnp.abs(x)))


def _layer_norm(r, g, b):
    mu = jnp.mean(r, axis=-1, keepdims=True)
    d = r - mu
    var = jnp.mean(d * d, axis=-1, keepdims=True)
    return d * lax.rsqrt(var + 1e-5) * g + b


def _mm_wt_kernel(a_ref, w_ref, side_ref, o_ref, side_b_ref, wb_ref):
    @pl.when(pl.program_id(1) == 0)
    def _():
        wb_ref[...] = w_ref[...].astype(BF16)

    o_ref[...] = lax.dot_general(a_ref[...], wb_ref[...], (((1,), (1,)), ((), ())),
                                 preferred_element_type=F32).astype(o_ref.dtype)
    side_b_ref[...] = side_ref[...].astype(BF16)


def _matmul_wt(a, w_t, n_blk, side_row0, side_rows, out_dtype, tm, tn, name):
    m, k = a.shape
    n_i = m // tm
    assert side_rows % (n_blk * n_i) == 0
    rs = side_rows // (n_blk * n_i)
    assert rs % 16 == 0 and side_row0 % rs == 0
    return pl.pallas_call(
        _mm_wt_kernel,
        grid=(n_blk, n_i),
        in_specs=[pl.BlockSpec((tm, k), lambda j, i: (i, 0)),
                  pl.BlockSpec((tn, k), lambda j, i: (j, 0)),
                  pl.BlockSpec((rs, k), lambda j, i: (side_row0 // rs + j * n_i + i, 0))],
        out_specs=[pl.BlockSpec((tm, tn), lambda j, i: (i, j)),
                   pl.BlockSpec((rs, k), lambda j, i: (j * n_i + i, 0))],
        out_shape=[jax.ShapeDtypeStruct((m, n_blk * tn), out_dtype),
                   jax.ShapeDtypeStruct((side_rows, k), BF16)],
        scratch_shapes=[pltpu.VMEM((tn, k), BF16)],
        compiler_params=_cparams(("arbitrary", "arbitrary"), VMEM_LIMIT_BIG_TILES),
        name=name,
    )(a, w_t, w_t)


def _xcast_dt_kernel(x_ref, w_ref, xb_ref, dt_ref):
    xb = x_ref[...].astype(BF16)
    xb_ref[...] = xb
    dt_ref[...] = lax.dot_general(xb, w_ref[...].astype(BF16), (((1,), (1,)), ((), ())),
                                  preferred_element_type=F32)


def _xcast_dt(x, w_t, blk):
    m, k = x.shape
    tm = 512
    return pl.pallas_call(
        _xcast_dt_kernel,
        grid=(m // tm,),
        in_specs=[pl.BlockSpec((tm, k), lambda i: (i, 0)),
                  pl.BlockSpec((LANES, k), lambda i: (blk, 0))],
        out_specs=[pl.BlockSpec((tm, k), lambda i: (i, 0)),
                   pl.BlockSpec((tm, LANES), lambda i: (i, 0))],
        out_shape=[jax.ShapeDtypeStruct((m, k), BF16), jax.ShapeDtypeStruct((m, LANES), F32)],
        compiler_params=_cparams(("arbitrary",)),
        name="x_cast_dt_proj",
    )(x, w_t)


def _dtprep_kernel(raw_ref, bias_ref, alog_ref, dt_ref, cs_ref):
    seq = raw_ref.shape[1]
    lc = SSM_CHUNK
    dt = _softplus(raw_ref[0] + bias_ref[...])
    dt_ref[0] = dt
    la = dt * (-jnp.exp(alog_ref[...]))
    row = lax.broadcasted_iota(jnp.int32, (lc, lc), 0)
    col = lax.broadcasted_iota(jnp.int32, (lc, lc), 1)
    t_low = jnp.where(row >= col, 1.0, 0.0).astype(F32)
    t_up = jnp.where(row <= col, 1.0, 0.0).astype(F32)
    fwd_lane = lax.broadcasted_iota(jnp.int32, (1, LANES), 1) < SSM_HEADS
    for c in range(seq // lc):
        lac = la[c * lc:(c + 1) * lc]
        f = jnp.dot(t_low, lac, preferred_element_type=F32, precision=lax.Precision.HIGHEST)
        b = jnp.dot(t_up, lac, preferred_element_type=F32, precision=lax.Precision.HIGHEST)
        cs_ref[0, c * lc:(c + 1) * lc, :] = jnp.where(fwd_lane, f, b)


def _dtprep(raw, bias, alog):
    bsz, seq, _ = raw.shape
    blk = pl.BlockSpec((1, seq, LANES), lambda b: (b, 0, 0))
    par = pl.BlockSpec((1, LANES), lambda b: (0, 0))
    return pl.pallas_call(
        _dtprep_kernel,
        grid=(bsz,),
        in_specs=[blk, par, par],
        out_specs=[blk, blk],
        out_shape=[jax.ShapeDtypeStruct(raw.shape, F32)] * 2,
        compiler_params=_cparams(("arbitrary",)),
        name="dt_prep",
    )(raw, bias, alog)


def _attn_kernel(lq1_ref, lk1_ref, lq2_ref, lk2_ref, sw_ref, cos_ref, sin_ref,
                 q_ref, k_ref, v_ref, o_ref, kr_ref, vb_ref, qs_ref, sa_ref, sb_ref, *, lambda_init, tq):
    seq = q_ref.shape[1]
    n_tiles = seq // tq
    lane = lax.broadcasted_iota(jnp.int32, (1, LANES), 1)
    first_half = (lane & (ATTN_HEAD_DIM // 2)) == 0
    comp1 = lane < ATTN_HEAD_DIM

    def rope(x, c, s):
        partner = jnp.where(first_half,
                            pltpu.roll(x, LANES - ATTN_HEAD_DIM // 2, 1),
                            pltpu.roll(x, ATTN_HEAD_DIM // 2, 1))
        return x * c + partner * s

    kr_ref[...] = rope(k_ref[0], cos_ref[...], sin_ref[...]).astype(BF16)
    vb_ref[:, 0:ATTN_V_DIM] = v_ref[0].astype(BF16)
    vb_ref[:, ATTN_V_DIM:] = jnp.where(lane == 0, 1.0, 0.0).astype(BF16) * jnp.ones((seq, 1), BF16)
    q = rope(q_ref[0], cos_ref[...], sin_ref[...]) * (ATTN_HEAD_DIM ** -0.5 * LOG2_E)
    for t in range(n_tiles):
        qt = q[t * tq:(t + 1) * tq]
        qs_ref[t, 0:tq, :] = jnp.where(comp1, qt, 0.0).astype(BF16)
        qs_ref[t, tq:2 * tq, :] = jnp.where(comp1, 0.0, qt).astype(BF16)

    lam = (jnp.exp(jnp.sum(lq1_ref[...] * lk1_ref[...], axis=-1, keepdims=True))
           - jnp.exp(jnp.sum(lq2_ref[...] * lk2_ref[...], axis=-1, keepdims=True)) + lambda_init)

    def scores(t, dst_ref):
        dst_ref[...] = lax.dot_general(qs_ref[t], kr_ref[...], (((1,), (1,)), ((), ())),
                                       preferred_element_type=F32)

    def finish(t, src_ref):
        s = src_ref[...]
        p = jnp.exp2(s - jnp.max(s, axis=-1, keepdims=True)).astype(BF16)
        pv = jnp.dot(p, vb_ref[...], preferred_element_type=F32)
        inv = 1.0 / pv[:, ATTN_V_DIM:ATTN_V_DIM + 1]
        o = pv[:tq, :ATTN_V_DIM] * inv[:tq] - pv[tq:, :ATTN_V_DIM] * (lam * inv[tq:])
        o = o * lax.rsqrt(jnp.mean(o * o, axis=-1, keepdims=True) + 1e-6) * sw_ref[...]
        o_ref[0, pl.ds(pl.multiple_of(t * tq, tq), tq), :] = (o * (1.0 - lambda_init)).astype(o_ref.dtype)

    scores(0, sa_ref)

    def pair(j, carry):
        t = 2 * j
        scores(t + 1, sb_ref)
        finish(t, sa_ref)
        scores(t + 2, sa_ref)
        finish(t + 1, sb_ref)
        return carry

    for j in range(n_tiles // 2 - 1):
        pair(j, 0)
    scores(n_tiles - 1, sb_ref)
    finish(n_tiles - 2, sa_ref)
    finish(n_tiles - 1, sb_ref)


def _rope_tables(seq):
    half = ATTN_HEAD_DIM // 2
    inv_freq = ROPE_THETA ** (-jnp.arange(0, ATTN_HEAD_DIM, 2, dtype=F32) / ATTN_HEAD_DIM)
    ang = jnp.arange(seq, dtype=F32)[:, None] * inv_freq[None, :]
    cos, sin = jnp.cos(ang), jnp.sin(ang)
    reps = LANES // half
    sign = jnp.tile(jnp.concatenate([-jnp.ones((half,), F32), jnp.ones((half,), F32)]), reps // 2)
    return jnp.tile(cos, (1, reps)), jnp.tile(sin, (1, reps)) * sign[None, :]


def _attention(proj, lq1, lk1, lq2, lk2, subln_w, lambda_init, tq):
    bsz, seq, _ = proj.shape
    cos, sin = _rope_tables(seq)
    assert seq % (2 * tq) == 0 and seq // tq >= 2
    vec = pl.BlockSpec((1, ATTN_HEAD_DIM), lambda b, h: (0, 0))
    table = pl.BlockSpec((seq, LANES), lambda b, h: (0, 0))
    return pl.pallas_call(
        functools.partial(_attn_kernel, lambda_init=lambda_init, tq=tq),
        grid=(bsz, ATTN_HEADS),
        in_specs=[vec, vec, vec, vec,
                  pl.BlockSpec((1, LANES), lambda b, h: (0, 0)),
                  table, table,
                  pl.BlockSpec((1, seq, LANES), lambda b, h: (b, 0, h)),
                  pl.BlockSpec((1, seq, LANES), lambda b, h: (b, 0, ATTN_HEADS + h)),
                  pl.BlockSpec((1, seq, LANES), lambda b, h: (b, 0, 2 * ATTN_HEADS + h))],
        out_specs=pl.BlockSpec((1, seq, LANES), lambda b, h: (b, 0, h)),
        out_shape=jax.ShapeDtypeStruct((bsz, seq, ATTN_HEADS * ATTN_V_DIM), BF16),
        scratch_shapes=[pltpu.VMEM((seq, LANES), BF16), pltpu.VMEM((seq, ATTN_V_DIM + LANES), BF16),
                        pltpu.VMEM((seq // tq, 2 * tq, LANES), BF16),
                        pltpu.VMEM((2 * tq, seq), F32), pltpu.VMEM((2 * tq, seq), F32)],
        compiler_params=_cparams(("arbitrary", "arbitrary")),
        name="diff_attention",
    )(lq1, lk1, lq2, lk2, subln_w, cos, sin, proj, proj, proj)


def _ssd_kernel(z_ref, x_ref, b_ref, c_ref, dt_ref, cs_ref, cst_ref,
                cwx_ref, cwb_ref, cwc_ref, cbx_ref, cbb_ref, cbc_ref, dsk_ref, nw_ref,
                o_ref, pad_ref, xs_ref, bm_ref, cm_ref, y_ref, h_ref):
    seq = x_ref.shape[1]
    lc = SSM_CHUNK
    n_chunks = seq // lc
    row_tile = 256

    def conv_silu(in_ref, w_ref, bias_ref, out_ref, width):
        zeros = jnp.zeros((CONV_PAD_ROWS, width), F32)
        pad_ref[0:CONV_PAD_ROWS, 0:width] = zeros
        pad_ref[CONV_PAD_ROWS + seq:2 * CONV_PAD_ROWS + seq, 0:width] = zeros
        pad_ref[CONV_PAD_ROWS:CONV_PAD_ROWS + seq, 0:width] = in_ref[0]
        half = (CONV_WIDTH - 1) // 2
        for t in range(seq // row_tile):
            acc = jnp.broadcast_to(bias_ref[...], (row_tile, width))
            for j in range(CONV_WIDTH):
                start = CONV_PAD_ROWS + t * row_tile + j - half
                acc = acc + pad_ref[start:start + row_tile, 0:width] * w_ref[j:j + 1, :]
            out_ref[t * row_tile:(t + 1) * row_tile, :] = _silu(acc)

    conv_silu(x_ref, cwx_ref, cbx_ref, xs_ref, GROUP_W)
    conv_silu(b_ref, cwb_ref, cbb_ref, bm_ref, SSM_STATE)
    conv_silu(c_ref, cwc_ref, cbc_ref, cm_ref, SSM_STATE)

    head_of_lane = lax.broadcasted_iota(jnp.int32, (1, GROUP_W), 1) // SSM_HEAD_DIM
    row = lax.broadcasted_iota(jnp.int32, (lc, lc), 0)
    col = lax.broadcasted_iota(jnp.int32, (lc, lc), 1)

    def expand(cols, off):
        out = cols[:, off + HEADS_PER_GROUP - 1:off + HEADS_PER_GROUP]
        for r in range(HEADS_PER_GROUP - 2, -1, -1):
            out = jnp.where(head_of_lane == r, cols[:, off + r:off + r + 1], out)
        return out

    def run_direction(reverse):
        off = HEADS_PER_GROUP if reverse else 0
        mask = (row <= col) if reverse else (row >= col)
        edge = 0 if reverse else lc - 1
        h_ref[...] = jnp.zeros_like(h_ref)

        def body(ci, carry):
            c = (n_chunks - 1 - ci) if reverse else ci
            r0 = pl.multiple_of(c * lc, lc)
            xc = xs_ref[pl.ds(r0, lc), :]
            bc = bm_ref[pl.ds(r0, lc), :]
            cc = cm_ref[pl.ds(r0, lc), :].astype(BF16)
            dtc = dt_ref[0, 0, pl.ds(r0, lc), :]
            csc = cs_ref[0, 0, pl.ds(r0, lc), :]
            cst = cst_ref[0, 0, c]
            cs_e = expand(csc, off)
            edge_e = expand(csc[edge:edge + 1, :], off)
            xdt = xc * expand(dtc, off)
            xdt_b = xdt.astype(BF16)
            cb = lax.dot_general(cc, bc.astype(BF16), (((1,), (1,)), ((), ())),
                                 preferred_element_type=F32)
            y = jnp.zeros((lc, GROUP_W), F32)
            for r in range(HEADS_PER_GROUP):
                diff = csc[:, off + r:off + r + 1] - cst[off + r:off + r + 1, :]
                decay = jnp.exp(jnp.where(mask, diff, -jnp.inf))
                yr = jnp.dot((cb * decay).astype(BF16), xdt_b, preferred_element_type=F32)
                y = jnp.where(head_of_lane == r, yr, y)
            h_t = h_ref[...]
            y = y + jnp.dot(cc, h_t.astype(BF16), preferred_element_type=F32) * jnp.exp(cs_e)
            new_state = jnp.dot(bc.T.astype(BF16), (xdt * jnp.exp(edge_e - cs_e)).astype(BF16),
                                preferred_element_type=F32)
            h_ref[...] = h_t * jnp.exp(edge_e) + new_state
            if reverse:
                y_ref[pl.ds(r0, lc), :] += y
            else:
                y_ref[pl.ds(r0, lc), :] = y
            return carry

        lax.fori_loop(0, n_chunks, body, 0, unroll=True)

    run_direction(False)
    run_direction(True)

    for t in range(seq // row_tile):
        rows = slice(t * row_tile, (t + 1) * row_tile)
        y = y_ref[rows, :] + xs_ref[rows, :] * dsk_ref[...]
        y = y * _silu(z_ref[0, rows, :])
        y = y * lax.rsqrt(jnp.mean(y * y, axis=-1, keepdims=True) + 1e-6) * nw_ref[...]
        o_ref[0, rows, :] = y.astype(o_ref.dtype)


def _ssd(proj, dtg, csg, cstg, conv_w, conv_b, dskip_e, norm_w, col0_z, col0_xbc):
    bsz, seq, _ = proj.shape
    d_ssm = SSM_HEADS * SSM_HEAD_DIM
    gn = SSM_GROUPS * SSM_STATE
    n_chunks = seq // SSM_CHUNK
    zb, xb = col0_z // GROUP_W, col0_xbc // GROUP_W
    bb, cb = (col0_xbc + d_ssm) // SSM_STATE, (col0_xbc + d_ssm + gn) // SSM_STATE
    wb, wc = d_ssm // SSM_STATE, (d_ssm + gn) // SSM_STATE
    n_dir_heads = 2 * HEADS_PER_GROUP
    return pl.pallas_call(
        _ssd_kernel,
        grid=(bsz, SSM_GROUPS),
        in_specs=[pl.BlockSpec((1, seq, GROUP_W), lambda b, g: (b, 0, zb + g)),
                  pl.BlockSpec((1, seq, GROUP_W), lambda b, g: (b, 0, xb + g)),
                  pl.BlockSpec((1, seq, SSM_STATE), lambda b, g: (b, 0, bb + g)),
                  pl.BlockSpec((1, seq, SSM_STATE), lambda b, g: (b, 0, cb + g)),
                  pl.BlockSpec((1, 1, seq, n_dir_heads), lambda b, g: (b, g, 0, 0)),
                  pl.BlockSpec((1, 1, seq, n_dir_heads), lambda b, g: (b, g, 0, 0)),
                  pl.BlockSpec((1, 1, n_chunks, n_dir_heads, SSM_CHUNK), lambda b, g: (b, g, 0, 0, 0)),
                  pl.BlockSpec((CONV_WIDTH, GROUP_W), lambda b, g: (0, g)),
                  pl.BlockSpec((CONV_WIDTH, SSM_STATE), lambda b, g: (0, wb + g)),
                  pl.BlockSpec((CONV_WIDTH, SSM_STATE), lambda b, g: (0, wc + g)),
                  pl.BlockSpec((1, GROUP_W), lambda b, g: (0, g)),
                  pl.BlockSpec((1, SSM_STATE), lambda b, g: (0, wb + g)),
                  pl.BlockSpec((1, SSM_STATE), lambda b, g: (0, wc + g)),
                  pl.BlockSpec((1, GROUP_W), lambda b, g: (0, g)),
                  pl.BlockSpec((1, GROUP_W), lambda b, g: (0, g))],
        out_specs=pl.BlockSpec((1, seq, GROUP_W), lambda b, g: (b, 0, g)),
        out_shape=jax.ShapeDtypeStruct((bsz, seq, d_ssm), BF16),
        scratch_shapes=[pltpu.VMEM((seq + 2 * CONV_PAD_ROWS, GROUP_W), F32),
                        pltpu.VMEM((seq, GROUP_W), F32),
                        pltpu.VMEM((seq, SSM_STATE), F32),
                        pltpu.VMEM((seq, SSM_STATE), F32),
                        pltpu.VMEM((seq, GROUP_W), F32),
                        pltpu.VMEM((SSM_STATE, GROUP_W), F32)],
        compiler_params=_cparams(("arbitrary", "arbitrary")),
        name="ssd",
    )(proj, proj, proj, proj, dtg, csg, cstg, conv_w, conv_w, conv_w, conv_b, conv_b, conv_b,
      dskip_e, norm_w)


def _merge_kernel(x_ref, wga_ref, wgs_ref, bga_ref, bgs_ref, ya_ref, ys_ref, wba_ref, wbs_ref, u_ref,
                  wba_b_ref, wbs_b_ref):
    @pl.when(pl.program_id(1) == 0)
    def _():
        wba_b_ref[...] = wba_ref[...].astype(BF16)
        wbs_b_ref[...] = wbs_ref[...].astype(BF16)

    x = x_ref[...]
    nt = (((1,), (1,)), ((), ()))
    ga = _sigmoid(lax.dot_general(x, wga_ref[...], nt, preferred_element_type=F32) + bga_ref[...])
    gs = _sigmoid(lax.dot_general(x, wgs_ref[...], nt, preferred_element_type=F32) + bgs_ref[...])
    ba = jnp.dot(ya_ref[...], wba_b_ref[...], preferred_element_type=F32)
    bs = jnp.dot(ys_ref[...], wbs_b_ref[...], preferred_element_type=F32)
    u_ref[...] = (ga * ba + gs * bs).astype(u_ref.dtype)


def _merge(xb, w_gate, b_gate, y_attn, y_ssm, w_branch, tm, tn):
    m, d = xb.shape
    n = w_branch.shape[1]
    ka, ks = y_attn.shape[1], y_ssm.shape[1]
    nj = n // tn
    return pl.pallas_call(
        _merge_kernel,
        grid=(nj, m // tm),
        in_specs=[pl.BlockSpec((tm, d), lambda j, i: (i, 0)),
                  pl.BlockSpec((tn, d), lambda j, i: (j, 0)),
                  pl.BlockSpec((tn, d), lambda j, i: (nj + j, 0)),
                  pl.BlockSpec((1, tn), lambda j, i: (0, j)),
                  pl.BlockSpec((1, tn), lambda j, i: (0, nj + j)),
                  pl.BlockSpec((tm, ka), lambda j, i: (i, 0)),
                  pl.BlockSpec((tm, ks), lambda j, i: (i, 0)),
                  pl.BlockSpec((ka, tn), lambda j, i: (0, j)),
                  pl.BlockSpec((ks, tn), lambda j, i: (ka // ks, j))],
        out_specs=pl.BlockSpec((tm, tn), lambda j, i: (i, j)),
        out_shape=jax.ShapeDtypeStruct((m, n), BF16),
        scratch_shapes=[pltpu.VMEM((ka, tn), BF16), pltpu.VMEM((ks, tn), BF16)],
        compiler_params=_cparams(("arbitrary", "arbitrary"), VMEM_LIMIT_BIG_TILES),
        name="gated_merge",
    )(xb, w_gate, w_gate, b_gate, b_gate, y_attn, y_ssm, w_branch, w_branch)


def _outproj_kernel(u_ref, w_ref, x_ref, g_ref, b_ref, wr_ref, x1_ref, lg_ref):
    j = pl.program_id(1)
    tn = w_ref.shape[1]
    n_slabs = x1_ref.shape[1] // tn
    part = jnp.dot(u_ref[...], w_ref[...], preferred_element_type=F32)

    for slab in range(n_slabs):
        @pl.when(j == slab)
        def _(slab=slab):
            x1_ref[:, slab * tn:(slab + 1) * tn] = part

    @pl.when(j == n_slabs - 1)
    def _():
        x1 = _layer_norm(ALPHA * x_ref[...] + x1_ref[...], g_ref[...], b_ref[...])
        x1_ref[...] = x1
        lg_ref[...] = jnp.dot(x1.astype(BF16), wr_ref[...], preferred_element_type=F32)


def _outproj_ln(u, w_out, x, g, b, w_router_p, tm, tn):
    m, kdim = u.shape
    d = w_out.shape[1]
    row = pl.BlockSpec((tm, d), lambda i, j: (i, 0))
    par = pl.BlockSpec((1, d), lambda i, j: (0, 0))
    return pl.pallas_call(
        _outproj_kernel,
        grid=(m // tm, d // tn),
        in_specs=[pl.BlockSpec((tm, kdim), lambda i, j: (i, 0)),
                  pl.BlockSpec((kdim, tn), lambda i, j: (0, j)),
                  pl.BlockSpec((tm, d), lambda i, j: (i, 0)),
                  par, par,
                  pl.BlockSpec((d, LANES), lambda i, j: (0, 0), pipeline_mode=pl.Buffered(1))],
        out_specs=[row, pl.BlockSpec((tm, LANES), lambda i, j: (i, 0))],
        out_shape=[jax.ShapeDtypeStruct((m, d), F32), jax.ShapeDtypeStruct((m, LANES), F32)],
        compiler_params=_cparams(("arbitrary", "arbitrary"), VMEM_LIMIT_BIG_TILES),
        name="out_proj_ln1",
    )(u, w_out, x, g, b, w_router_p)


def _routing_kernel(lg_ref, slot_ref, slott_ref, gslot_ref, aff_ref, *, cap):
    seq = lg_ref.shape[1]
    blk = 256
    lg = lg_ref[0]
    valid = lax.broadcasted_iota(jnp.int32, lg.shape, 1) < N_EXPERTS
    lgm = jnp.where(valid, lg, -jnp.inf)
    ex = jnp.exp(lgm - jnp.max(lgm, axis=-1, keepdims=True))
    aff = ex / jnp.sum(ex, axis=-1, keepdims=True)
    aff_t = aff.T[0:N_EXPERTS]
    aff_ref[...] = aff_t
    bits = lax.bitcast_convert_type(aff_t, jnp.int32)

    def count(m):
        return jnp.sum(jnp.where(m, 1.0, 0.0), axis=-1, keepdims=True)

    def search(i, thr):
        cand = thr | jnp.left_shift(jnp.int32(1), 30 - i)
        return jnp.where(count(bits >= cand) >= cap, cand, thr)

    thr = lax.fori_loop(0, 31, search, jnp.zeros((N_EXPERTS, 1), jnp.int32))
    gt = bits > thr
    eq = bits == thr

    r_i = lax.broadcasted_iota(jnp.int32, (blk, blk), 0)
    c_i = lax.broadcasted_iota(jnp.int32, (blk, blk), 1)
    before = jnp.where(r_i < c_i, 1.0, 0.0).astype(BF16)

    def excl_cumsum(m):
        mf = jnp.where(m, 1.0, 0.0)
        carry = jnp.zeros((N_EXPERTS, 1), F32)
        parts = []
        for k in range(seq // blk):
            piece = mf[:, k * blk:(k + 1) * blk]
            parts.append(jnp.dot(piece.astype(BF16), before, preferred_element_type=F32) + carry)
            carry = carry + jnp.sum(piece, axis=-1, keepdims=True)
        return jnp.concatenate(parts, axis=-1)

    need = cap - count(gt)
    sel = gt | (eq & (excl_cumsum(eq) < need))
    slot = jnp.where(sel, excl_cumsum(sel), -1.0)
    slot_ref[0] = slot
    pad = jnp.full((LANES - N_EXPERTS, seq), -1.0, F32)
    slott_ref[0] = jnp.concatenate([slot, pad], axis=0).T

    j_iota = lax.broadcasted_iota(jnp.int32, (cap, seq), 0).astype(F32)

    def gate_of_slot(e, carry):
        hit = slot_ref[0, pl.ds(e, 1), :] == j_iota
        gslot_ref[0, e] = jnp.sum(jnp.where(hit, aff_ref[pl.ds(e, 1), :], 0.0), axis=-1, keepdims=True)
        return carry

    lax.fori_loop(0, N_EXPERTS, gate_of_slot, 0)


def _routing(logits, cap):
    bsz, seq, _ = logits.shape
    return pl.pallas_call(
        functools.partial(_routing_kernel, cap=cap),
        grid=(bsz,),
        in_specs=[pl.BlockSpec((1, seq, LANES), lambda b: (b, 0, 0))],
        out_specs=[pl.BlockSpec((1, N_EXPERTS, seq), lambda b: (b, 0, 0)),
                   pl.BlockSpec((1, seq, LANES), lambda b: (b, 0, 0)),
                   pl.BlockSpec((1, N_EXPERTS, cap, 1), lambda b: (b, 0, 0, 0))],
        out_shape=[jax.ShapeDtypeStruct((bsz, N_EXPERTS, seq), F32),
                   jax.ShapeDtypeStruct((bsz, seq, LANES), F32),
                   jax.ShapeDtypeStruct((bsz, N_EXPERTS, cap, 1), F32)],
        scratch_shapes=[pltpu.VMEM((N_EXPERTS, seq), F32)],
        compiler_params=_cparams(("arbitrary",)),
        name="routing",
    )(logits)


def _gather_kernel(slot_ref, x_ref, o_ref, pick_ref, *, cap):
    seq = x_ref.shape[1]
    td = x_ref.shape[2]

    @pl.when(pl.program_id(1) == 0)
    def _():
        j_iota = lax.broadcasted_iota(jnp.int32, (cap, seq), 0).astype(F32)

        def one_expert(e, carry):
            hit = slot_ref[0, pl.ds(e, 1), :] == j_iota
            pick_ref[pl.ds(pl.multiple_of(e * cap, cap), cap), :] = jnp.where(hit, 1.0, 0.0).astype(BF16)
            return carry

        lax.fori_loop(0, N_EXPERTS, one_expert, 0)

    rows = jnp.dot(pick_ref[...], x_ref[0].astype(BF16), preferred_element_type=F32)
    o_ref[...] = rows.reshape(N_EXPERTS, cap, td).astype(o_ref.dtype)


def _gather(slot, x1, cap, td):
    bsz, seq, d = x1.shape
    return pl.pallas_call(
        functools.partial(_gather_kernel, cap=cap),
        grid=(bsz, d // td),
        in_specs=[pl.BlockSpec((1, N_EXPERTS, seq), lambda b, j: (b, 0, 0)),
                  pl.BlockSpec((1, seq, td), lambda b, j: (b, 0, j))],
        out_specs=pl.BlockSpec((N_EXPERTS, cap, td), lambda b, j: (0, b, j)),
        out_shape=jax.ShapeDtypeStruct((N_EXPERTS, bsz * cap, d), BF16),
        scratch_shapes=[pltpu.VMEM((N_EXPERTS * cap, seq), BF16)],
        compiler_params=_cparams(("arbitrary", "arbitrary")),
        name="moe_gather",
    )(slot, x1)


def _gateup_kernel(xg_ref, wg_ref, wu_ref, h_ref):
    xg = xg_ref[0]
    g = jnp.dot(xg, wg_ref[0].astype(BF16), preferred_element_type=F32)
    u = jnp.dot(xg, wu_ref[0].astype(BF16), preferred_element_type=F32)
    h_ref[0] = (_silu(g) * u).astype(h_ref.dtype)


def _gateup(xg, w_gate_e, w_up_e, tf):
    n_e, rows, d = xg.shape
    ff = w_gate_e.shape[2]
    wspec = pl.BlockSpec((1, d, tf), lambda e, f: (e, 0, f))
    return pl.pallas_call(
        _gateup_kernel,
        grid=(n_e, ff // tf),
        in_specs=[pl.BlockSpec((1, rows, d), lambda e, f: (e, 0, 0)), wspec, wspec],
        out_specs=pl.BlockSpec((1, rows, tf), lambda e, f: (e, 0, f)),
        out_shape=jax.ShapeDtypeStruct((n_e, rows, ff), BF16),
        compiler_params=_cparams(("arbitrary", "arbitrary")),
        name="moe_gate_up",
    )(xg, w_gate_e, w_up_e)


def _down_kernel(h_ref, wd_ref, gs_ref, y_ref):
    y = jnp.dot(h_ref[0], wd_ref[0].astype(BF16), preferred_element_type=F32)
    y_ref[0] = (y * gs_ref[0]).astype(y_ref.dtype)


def _down(h, w_down_e, gslot, td):
    n_e, rows, ff = h.shape
    d = w_down_e.shape[2]
    return pl.pallas_call(
        _down_kernel,
        grid=(n_e, d // td),
        in_specs=[pl.BlockSpec((1, rows, ff), lambda e, j: (e, 0, 0)),
                  pl.BlockSpec((1, ff, td), lambda e, j: (e, 0, j)),
                  pl.BlockSpec((1, rows, 1), lambda e, j: (e, 0, 0))],
        out_specs=pl.BlockSpec((1, rows, td), lambda e, j: (e, 0, j)),
        out_shape=jax.ShapeDtypeStruct((n_e, rows, d), BF16),
        compiler_params=_cparams(("arbitrary", "arbitrary")),
        name="moe_down",
    )(h, w_down_e, gslot)


def _scatter_kernel(slott_ref, yg_ref, x1_ref, g_ref, b_ref, o_ref, put_ref, *, cap):
    dj = pl.program_id(2)
    ts = o_ref.shape[1]
    td = yg_ref.shape[2]
    n_slabs = o_ref.shape[2] // td

    @pl.when(dj == 0)
    def _():
        st = slott_ref[0]
        lane = lax.broadcasted_iota(jnp.int32, st.shape, 1)
        j_iota = lax.broadcasted_iota(jnp.int32, (ts, cap), 1).astype(F32)
        for e in range(N_EXPERTS):
            col = jnp.sum(jnp.where(lane == e, st, 0.0), axis=-1, keepdims=True)
            put_ref[:, e * cap:(e + 1) * cap] = jnp.where(col == j_iota, 1.0, 0.0).astype(BF16)

    part = jnp.dot(put_ref[...], yg_ref[...].reshape(N_EXPERTS * cap, td), preferred_element_type=F32)

    for slab in range(n_slabs):
        @pl.when(dj == slab)
        def _(slab=slab):
            o_ref[0, :, slab * td:(slab + 1) * td] = part

    @pl.when(dj == n_slabs - 1)
    def _():
        o_ref[0] = _layer_norm(ALPHA * x1_ref[0] + o_ref[0], g_ref[...], b_ref[...])


def _scatter_ln(slot_t, yg, x1, g, b, cap, ts, td):
    bsz, seq, d = x1.shape
    par = pl.BlockSpec((1, d), lambda bi, i, j: (0, 0))
    return pl.pallas_call(
        functools.partial(_scatter_kernel, cap=cap),
        grid=(bsz, seq // ts, d // td),
        in_specs=[pl.BlockSpec((1, ts, LANES), lambda bi, i, j: (bi, i, 0)),
                  pl.BlockSpec((N_EXPERTS, cap, td), lambda bi, i, j: (0, bi, j)),
                  pl.BlockSpec((1, ts, d), lambda bi, i, j: (bi, i, 0)),
                  par, par],
        out_specs=pl.BlockSpec((1, ts, d), lambda bi, i, j: (bi, i, 0)),
        out_shape=jax.ShapeDtypeStruct((bsz, seq, d), F32),
        scratch_shapes=[pltpu.VMEM((ts, N_EXPERTS * cap), BF16)],
        compiler_params=_cparams(("arbitrary", "arbitrary", "arbitrary"), VMEM_LIMIT_BIG_TILES),
        name="moe_scatter_ln2",
    )(slot_t, yg, x1, g, b)


def _group_heads(t):
    bsz, seq, _ = t.shape
    t = t[:, :, :2 * SSM_HEADS].reshape(bsz, seq, 2, SSM_GROUPS, HEADS_PER_GROUP)
    return jnp.transpose(t, (0, 3, 1, 2, 4)).reshape(bsz, SSM_GROUPS, seq, 2 * HEADS_PER_GROUP)


def _layer(x, w_in, b_gate, lq1, lk1, lq2, lk2, subln_w, conv_w, conv_b, dtb_f, dtb_b, alog_f, alog_b,
           d_skip, ssm_norm_w, w_branch, w_out, ln1_g, ln1_b, w_router, w_gate_e, w_up_e, w_down_e,
           ln2_g, ln2_b, layer_idx):
    bsz, seq, d = x.shape
    m = bsz * seq
    d_qk = ATTN_HEADS * 2 * ATTN_HEAD_DIM
    d_v = ATTN_HEADS * ATTN_V_DIM
    d_ssm = SSM_HEADS * SSM_HEAD_DIM
    d_conv = d_ssm + 2 * SSM_GROUPS * SSM_STATE
    n_main = 2 * d_qk + d_v + d_ssm + d_conv
    n_dt = 2 * SSM_HEADS
    lambda_init = 0.8 - 0.6 * math.exp(-0.3 * layer_idx)
    cap = CAPACITY_FACTOR * seq // N_EXPERTS
    row = lambda v: v.reshape(1, -1)

    w_in_t = w_in.T
    xb, dt_raw = _xcast_dt(x.reshape(m, d), w_in_t, n_main // LANES)
    dt_raw = dt_raw.reshape(bsz, seq, LANES)

    tn_main = 768
    proj, w_gate_t = _matmul_wt(xb, w_in_t, n_main // tn_main, n_main + n_dt, 2 * d, F32, 1024, tn_main,
                                "in_proj")
    proj = proj.reshape(bsz, seq, n_main)

    lane_pad = lambda a, bvec: jnp.pad(jnp.concatenate([a, bvec]), (0, LANES - n_dt)).reshape(1, LANES)
    dt, cs = _dtprep(dt_raw, lane_pad(dtb_f, dtb_b), lane_pad(alog_f, alog_b))
    dtg, csg = _group_heads(dt), _group_heads(cs)
    n_chunks = seq // SSM_CHUNK
    cstg = jnp.transpose(csg.reshape(bsz, SSM_GROUPS, n_chunks, SSM_CHUNK, 2 * HEADS_PER_GROUP),
                         (0, 1, 2, 4, 3))

    y_attn = _attention(proj, row(lq1), row(lk1), row(lq2), row(lk2), row(subln_w), lambda_init, 256)
    y_ssm = _ssd(proj, dtg, csg, cstg, conv_w, row(conv_b), row(jnp.repeat(d_skip, SSM_HEAD_DIM)),
                 row(ssm_norm_w), 2 * d_qk + d_v, 2 * d_qk + d_v + d_ssm)

    u = _merge(xb, w_gate_t, row(b_gate), y_attn.reshape(m, d_v), y_ssm.reshape(m, d_ssm),
               w_branch, 512, 512)
    w_router_p = jnp.pad(w_router, ((0, 0), (0, LANES - N_EXPERTS))).astype(BF16)
    x1, logits = _outproj_ln(u, w_out.astype(BF16), x.reshape(m, d), row(ln1_g), row(ln1_b),
                             w_router_p, 512, 512)
    x1 = x1.reshape(bsz, seq, d)

    slot, slot_t, gslot = _routing(logits.reshape(bsz, seq, LANES), cap)
    xg = _gather(slot, x1, cap, 512)
    h = _gateup(xg, w_gate_e, w_up_e, 256)
    gslot_e = jnp.transpose(gslot, (1, 0, 2, 3)).reshape(N_EXPERTS, bsz * cap, 1)
    yg = _down(h, w_down_e, gslot_e, 1024)
    return _scatter_ln(slot_t, yg, x1, row(ln2_g), row(ln2_b), cap, 512, 512)


def kernel(x, w_in, b_gate, lambda_q1, lambda_k1, lambda_q2, lambda_k2, attn_subln_w, conv_w, conv_b,
           dt_bias_fwd, dt_bias_bwd, a_log_fwd, a_log_bwd, d_skip, ssm_norm_w, w_branch, w_out,
           ln1_g, ln1_b, w_router, w_gate_e, w_up_e, w_down_e, ln2_g, ln2_b):
    for l in range(w_in.shape[0]):
        x = _layer(x, w_in[l], b_gate[l], lambda_q1[l], lambda_k1[l], lambda_q2[l], lambda_k2[l],
                   attn_subln_w[l], conv_w[l], conv_b[l], dt_bias_fwd[l], dt_bias_bwd[l],
                   a_log_fwd[l], a_log_bwd[l], d_skip[l], ssm_norm_w[l], w_branch[l], w_out[l],
                   ln1_g[l], ln1_b[l], w_router[l], w_gate_e[l], w_up_e[l], w_down_e[l],
                   ln2_g[l], ln2_b[l], l)
    return x
```

```python
import functools
import math

import jax
import jax.numpy as jnp
from jax import lax
from jax.experimental import pallas as pl
from jax.experimental.pallas import tpu as pltpu

F32 = jnp.float32
BF16 = jnp.bfloat16

ATTN_HEADS = 16
ATTN_HEAD_DIM = 64
ATTN_V_DIM = 128
ROPE_THETA = 10000.0
LOG2_E = math.log2(math.e)
SSM_HEAD_DIM = 64
SSM_HEADS = 32
SSM_GROUPS = 8
SSM_STATE = 128
SSM_CHUNK = 128
CONV_WIDTH = 5
N_EXPERTS = 16
CAPACITY_FACTOR = 2
DEPTH = 1
ALPHA = (2.0 * DEPTH) ** 0.25

LANES = 128
SUBLANES = 8
VMEM_LIMIT = 56 * 1024 * 1024
VMEM_LIMIT_BIG_TILES = 63 * 1024 * 1024

HEADS_PER_GROUP = SSM_HEADS // SSM_GROUPS
GROUP_W = HEADS_PER_GROUP * SSM_HEAD_DIM
CONV_PAD_ROWS = SUBLANES


def _cparams(sem, vmem_limit=VMEM_LIMIT):
    return pltpu.CompilerParams(dimension_semantics=sem, vmem_limit_bytes=vmem_limit)


def _sigmoid(x):
    return 1.0 / (1.0 + jnp.exp(-x))


def _silu(x):
    return x * _sigmoid(x)


def _softplus(x):
    return jnp.maximum(x, 0.0) + jnp.log1p(jnp.exp(-jnp.abs(x)))


def _layer_norm(r, g, b):
    mu = jnp.mean(r, axis=-1, keepdims=True)
    d = r - mu
    var = jnp.mean(d * d, axis=-1, keepdims=True)
    return d * lax.rsqrt(var + 1e-5) * g + b


def _mm_wt_kernel(a_ref, w_ref, side_ref, o_ref, side_b_ref, wb_ref):
    @pl.when(pl.program_id(1) == 0)
    def _():
        wb_ref[...] = w_ref[...].astype(BF16)

    o_ref[...] = lax.dot_general(a_ref[...], wb_ref[...], (((1,), (1,)), ((), ())),
                                 preferred_element_type=F32).astype(o_ref.dtype)
    side_b_ref[...] = side_ref[...].astype(BF16)


def _matmul_wt(a, w_t, n_blk, side_row0, side_rows, out_dtype, tm, tn, name):
    m, k = a.shape
    n_i = m // tm
    assert side_rows % (n_blk * n_i) == 0
    rs = side_rows // (n_blk * n_i)
    assert rs % 16 == 0 and side_row0 % rs == 0
    return pl.pallas_call(
        _mm_wt_kernel,
        grid=(n_blk, n_i),
        in_specs=[pl.BlockSpec((tm, k), lambda j, i: (i, 0)),
                  pl.BlockSpec((tn, k), lambda j, i: (j, 0)),
                  pl.BlockSpec((rs, k), lambda j, i: (side_row0 // rs + j * n_i + i, 0))],
        out_specs=[pl.BlockSpec((tm, tn), lambda j, i: (i, j)),
                   pl.BlockSpec((rs, k), lambda j, i: (j * n_i + i, 0))],
        out_shape=[jax.ShapeDtypeStruct((m, n_blk * tn), out_dtype),
                   jax.ShapeDtypeStruct((side_rows, k), BF16)],
        scratch_shapes=[pltpu.VMEM((tn, k), BF16)],
        compiler_params=_cparams(("arbitrary", "arbitrary"), VMEM_LIMIT_BIG_TILES),
        name=name,
    )(a, w_t, w_t)


def _xcast_dt_kernel(x_ref, w_ref, xb_ref, dt_ref):
    xb = x_ref[...].astype(BF16)
    xb_ref[...] = xb
    dt_ref[...] = lax.dot_general(xb, w_ref[...].astype(BF16), (((1,), (1,)), ((), ())),
                                  preferred_element_type=F32)


def _xcast_dt(x, w_t, blk):
    m, k = x.shape
    tm = 512
    return pl.pallas_call(
        _xcast_dt_kernel,
        grid=(m // tm,),
        in_specs=[pl.BlockSpec((tm, k), lambda i: (i, 0)),
                  pl.BlockSpec((LANES, k), lambda i: (blk, 0))],
        out_specs=[pl.BlockSpec((tm, k), lambda i: (i, 0)),
                   pl.BlockSpec((tm, LANES), lambda i: (i, 0))],
        out_shape=[jax.ShapeDtypeStruct((m, k), BF16), jax.ShapeDtypeStruct((m, LANES), F32)],
        compiler_params=_cparams(("arbitrary",)),
        name="x_cast_dt_proj",
    )(x, w_t)


def _dtprep_kernel(raw_ref, bias_ref, alog_ref, dt_ref, cs_ref):
    seq = raw_ref.shape[1]
    lc = SSM_CHUNK
    dt = _softplus(raw_ref[0] + bias_ref[...])
    dt_ref[0] = dt
    la = dt * (-jnp.exp(alog_ref[...]))
    row = lax.broadcasted_iota(jnp.int32, (lc, lc), 0)
    col = lax.broadcasted_iota(jnp.int32, (lc, lc), 1)
    t_low = jnp.where(row >= col, 1.0, 0.0).astype(F32)
    t_up = jnp.where(row <= col, 1.0, 0.0).astype(F32)
    fwd_lane = lax.broadcasted_iota(jnp.int32, (1, LANES), 1) < SSM_HEADS
    for c in range(seq // lc):
        lac = la[c * lc:(c + 1) * lc]
        f = jnp.dot(t_low, lac, preferred_element_type=F32, precision=lax.Precision.HIGHEST)
        b = jnp.dot(t_up, lac, preferred_element_type=F32, precision=lax.Precision.HIGHEST)
        cs_ref[0, c * lc:(c + 1) * lc, :] = jnp.where(fwd_lane, f, b)


def _dtprep(raw, bias, alog):
    bsz, seq, _ = raw.shape
    blk = pl.BlockSpec((1, seq, LANES), lambda b: (b, 0, 0))
    par = pl.BlockSpec((1, LANES), lambda b: (0, 0))
    return pl.pallas_call(
        _dtprep_kernel,
        grid=(bsz,),
        in_specs=[blk, par, par],
        out_specs=[blk, blk],
        out_shape=[jax.ShapeDtypeStruct(raw.shape, F32)] * 2,
        compiler_params=_cparams(("arbitrary",)),
        name="dt_prep",
    )(raw, bias, alog)


def _attn_kernel(lq1_ref, lk1_ref, lq2_ref, lk2_ref, sw_ref, cos_ref, sin_ref,
                 q_ref, k_ref, v_ref, side_ref, o_ref, side_b_ref, kr_ref, vb_ref, qs_ref, sa_ref, sb_ref,
                 *, lambda_init, tq):
    side_b_ref[...] = side_ref[...].astype(BF16)
    seq = q_ref.shape[1]
    n_tiles = seq // tq
    lane = lax.broadcasted_iota(jnp.int32, (1, LANES), 1)
    first_half = (lane & (ATTN_HEAD_DIM // 2)) == 0
    comp1 = lane < ATTN_HEAD_DIM

    def rope(x, c, s):
        partner = jnp.where(first_half,
                            pltpu.roll(x, LANES - ATTN_HEAD_DIM // 2, 1),
                            pltpu.roll(x, ATTN_HEAD_DIM // 2, 1))
        return x * c + partner * s

    kr_ref[...] = rope(k_ref[0], cos_ref[...], sin_ref[...]).astype(BF16)
    vb_ref[:, 0:ATTN_V_DIM] = v_ref[0].astype(BF16)
    vb_ref[:, ATTN_V_DIM:] = jnp.where(lane == 0, 1.0, 0.0).astype(BF16) * jnp.ones((seq, 1), BF16)
    q = rope(q_ref[0], cos_ref[...], sin_ref[...]) * (ATTN_HEAD_DIM ** -0.5 * LOG2_E)
    for t in range(n_tiles):
        qt = q[t * tq:(t + 1) * tq]
        qs_ref[t, 0:tq, :] = jnp.where(comp1, qt, 0.0).astype(BF16)
        qs_ref[t, tq:2 * tq, :] = jnp.where(comp1, 0.0, qt).astype(BF16)

    lam = (jnp.exp(jnp.sum(lq1_ref[...] * lk1_ref[...], axis=-1, keepdims=True))
           - jnp.exp(jnp.sum(lq2_ref[...] * lk2_ref[...], axis=-1, keepdims=True)) + lambda_init)

    def scores(t, dst_ref):
        dst_ref[...] = lax.dot_general(qs_ref[t], kr_ref[...], (((1,), (1,)), ((), ())),
                                       preferred_element_type=F32)

    def finish(t, src_ref):
        s = src_ref[...]
        p = jnp.exp2(s - jnp.max(s, axis=-1, keepdims=True)).astype(BF16)
        pv = jnp.dot(p, vb_ref[...], preferred_element_type=F32)
        inv = 1.0 / pv[:, ATTN_V_DIM:ATTN_V_DIM + 1]
        o = pv[:tq, :ATTN_V_DIM] * inv[:tq] - pv[tq:, :ATTN_V_DIM] * (lam * inv[tq:])
        o = o * lax.rsqrt(jnp.mean(o * o, axis=-1, keepdims=True) + 1e-6) * sw_ref[...]
        o_ref[0, pl.ds(pl.multiple_of(t * tq, tq), tq), :] = (o * (1.0 - lambda_init)).astype(o_ref.dtype)

    scores(0, sa_ref)

    def pair(j, carry):
        t = 2 * j
        scores(t + 1, sb_ref)
        finish(t, sa_ref)
        scores(t + 2, sa_ref)
        finish(t + 1, sb_ref)
        return carry

    for j in range(n_tiles // 2 - 1):
        pair(j, 0)
    scores(n_tiles - 1, sb_ref)
    finish(n_tiles - 2, sa_ref)
    finish(n_tiles - 1, sb_ref)


def _rope_tables(seq):
    half = ATTN_HEAD_DIM // 2
    inv_freq = ROPE_THETA ** (-jnp.arange(0, ATTN_HEAD_DIM, 2, dtype=F32) / ATTN_HEAD_DIM)
    ang = jnp.arange(seq, dtype=F32)[:, None] * inv_freq[None, :]
    cos, sin = jnp.cos(ang), jnp.sin(ang)
    reps = LANES // half
    sign = jnp.tile(jnp.concatenate([-jnp.ones((half,), F32), jnp.ones((half,), F32)]), reps // 2)
    return jnp.tile(cos, (1, reps)), jnp.tile(sin, (1, reps)) * sign[None, :]


def _attention(proj, lq1, lk1, lq2, lk2, subln_w, w_side, lambda_init, tq):
    bsz, seq, _ = proj.shape
    cos, sin = _rope_tables(seq)
    assert seq % (2 * tq) == 0 and seq // tq >= 2
    side_rows, side_cols = w_side.shape
    n_steps = bsz * ATTN_HEADS
    assert side_rows % n_steps == 0 and (side_rows // n_steps) % 16 == 0
    rs = side_rows // n_steps
    side_spec = pl.BlockSpec((rs, side_cols), lambda b, h: (b * ATTN_HEADS + h, 0))
    vec = pl.BlockSpec((1, ATTN_HEAD_DIM), lambda b, h: (0, 0))
    table = pl.BlockSpec((seq, LANES), lambda b, h: (0, 0))
    return pl.pallas_call(
        functools.partial(_attn_kernel, lambda_init=lambda_init, tq=tq),
        grid=(bsz, ATTN_HEADS),
        in_specs=[vec, vec, vec, vec,
                  pl.BlockSpec((1, LANES), lambda b, h: (0, 0)),
                  table, table,
                  pl.BlockSpec((1, seq, LANES), lambda b, h: (b, 0, h)),
                  pl.BlockSpec((1, seq, LANES), lambda b, h: (b, 0, ATTN_HEADS + h)),
                  pl.BlockSpec((1, seq, LANES), lambda b, h: (b, 0, 2 * ATTN_HEADS + h)),
                  side_spec],
        out_specs=[pl.BlockSpec((1, seq, LANES), lambda b, h: (b, 0, h)), side_spec],
        out_shape=[jax.ShapeDtypeStruct((bsz, seq, ATTN_HEADS * ATTN_V_DIM), BF16),
                   jax.ShapeDtypeStruct((side_rows, side_cols), BF16)],
        scratch_shapes=[pltpu.VMEM((seq, LANES), BF16), pltpu.VMEM((seq, ATTN_V_DIM + LANES), BF16),
                        pltpu.VMEM((seq // tq, 2 * tq, LANES), BF16),
                        pltpu.VMEM((2 * tq, seq), F32), pltpu.VMEM((2 * tq, seq), F32)],
        compiler_params=_cparams(("arbitrary", "arbitrary")),
        name="diff_attention",
    )(lq1, lk1, lq2, lk2, subln_w, cos, sin, proj, proj, proj, w_side)


def _ssd_kernel(z_ref, x_ref, b_ref, c_ref, dt_ref, cs_ref, cst_ref,
                cwx_ref, cwb_ref, cwc_ref, cbx_ref, cbb_ref, cbc_ref, dsk_ref, nw_ref,
                o_ref, pad_ref, xs_ref, bm_ref, cm_ref, y_ref, h_ref):
    seq = x_ref.shape[1]
    lc = SSM_CHUNK
    n_chunks = seq // lc
    row_tile = 256

    def conv_silu(in_ref, w_ref, bias_ref, out_ref, width):
        zeros = jnp.zeros((CONV_PAD_ROWS, width), F32)
        pad_ref[0:CONV_PAD_ROWS, 0:width] = zeros
        pad_ref[CONV_PAD_ROWS + seq:2 * CONV_PAD_ROWS + seq, 0:width] = zeros
        pad_ref[CONV_PAD_ROWS:CONV_PAD_ROWS + seq, 0:width] = in_ref[0]
        half = (CONV_WIDTH - 1) // 2
        for t in range(seq // row_tile):
            acc = jnp.broadcast_to(bias_ref[...], (row_tile, width))
            for j in range(CONV_WIDTH):
                start = CONV_PAD_ROWS + t * row_tile + j - half
                acc = acc + pad_ref[start:start + row_tile, 0:width] * w_ref[j:j + 1, :]
            out_ref[t * row_tile:(t + 1) * row_tile, :] = _silu(acc)

    conv_silu(x_ref, cwx_ref, cbx_ref, xs_ref, GROUP_W)
    conv_silu(b_ref, cwb_ref, cbb_ref, bm_ref, SSM_STATE)
    conv_silu(c_ref, cwc_ref, cbc_ref, cm_ref, SSM_STATE)

    head_of_lane = lax.broadcasted_iota(jnp.int32, (1, GROUP_W), 1) // SSM_HEAD_DIM
    row = lax.broadcasted_iota(jnp.int32, (lc, lc), 0)
    col = lax.broadcasted_iota(jnp.int32, (lc, lc), 1)

    def expand(cols, off):
        out = cols[:, off + HEADS_PER_GROUP - 1:off + HEADS_PER_GROUP]
        for r in range(HEADS_PER_GROUP - 2, -1, -1):
            out = jnp.where(head_of_lane == r, cols[:, off + r:off + r + 1], out)
        return out

    def run_direction(reverse):
        off = HEADS_PER_GROUP if reverse else 0
        mask = (row <= col) if reverse else (row >= col)
        edge = 0 if reverse else lc - 1
        h_ref[...] = jnp.zeros_like(h_ref)

        def body(ci, carry):
            c = (n_chunks - 1 - ci) if reverse else ci
            r0 = pl.multiple_of(c * lc, lc)
            xc = xs_ref[pl.ds(r0, lc), :]
            bc = bm_ref[pl.ds(r0, lc), :]
            cc = cm_ref[pl.ds(r0, lc), :].astype(BF16)
            dtc = dt_ref[0, 0, pl.ds(r0, lc), :]
            csc = cs_ref[0, 0, pl.ds(r0, lc), :]
            cst = cst_ref[0, 0, c]
            cs_e = expand(csc, off)
            edge_e = expand(csc[edge:edge + 1, :], off)
            xdt = xc * expand(dtc, off)
            xdt_b = xdt.astype(BF16)
            cb = lax.dot_general(cc, bc.astype(BF16), (((1,), (1,)), ((), ())),
                                 preferred_element_type=F32)
            y = jnp.zeros((lc, GROUP_W), F32)
            for r in range(HEADS_PER_GROUP):
                diff = csc[:, off + r:off + r + 1] - cst[off + r:off + r + 1, :]
                decay = jnp.exp(jnp.where(mask, diff, -jnp.inf))
                yr = jnp.dot((cb * decay).astype(BF16), xdt_b, preferred_element_type=F32)
                y = jnp.where(head_of_lane == r, yr, y)
            h_t = h_ref[...]
            y = y + jnp.dot(cc, h_t.astype(BF16), preferred_element_type=F32) * jnp.exp(cs_e)
            new_state = jnp.dot(bc.T.astype(BF16), (xdt * jnp.exp(edge_e - cs_e)).astype(BF16),
                                preferred_element_type=F32)
            h_ref[...] = h_t * jnp.exp(edge_e) + new_state
            if reverse:
                y_ref[pl.ds(r0, lc), :] += y
            else:
                y_ref[pl.ds(r0, lc), :] = y
            return carry

        lax.fori_loop(0, n_chunks, body, 0, unroll=True)

    run_direction(False)
    run_direction(True)

    for t in range(seq // row_tile):
        rows = slice(t * row_tile, (t + 1) * row_tile)
        y = y_ref[rows, :] + xs_ref[rows, :] * dsk_ref[...]
        y = y * _silu(z_ref[0, rows, :])
        y = y * lax.rsqrt(jnp.mean(y * y, axis=-1, keepdims=True) + 1e-6) * nw_ref[...]
        o_ref[0, rows, :] = y.astype(o_ref.dtype)


def _ssd(proj, dtg, csg, cstg, conv_w, conv_b, dskip_e, norm_w, col0_z, col0_xbc):
    bsz, seq, _ = proj.shape
    d_ssm = SSM_HEADS * SSM_HEAD_DIM
    gn = SSM_GROUPS * SSM_STATE
    n_chunks = seq // SSM_CHUNK
    zb, xb = col0_z // GROUP_W, col0_xbc // GROUP_W
    bb, cb = (col0_xbc + d_ssm) // SSM_STATE, (col0_xbc + d_ssm + gn) // SSM_STATE
    wb, wc = d_ssm // SSM_STATE, (d_ssm + gn) // SSM_STATE
    n_dir_heads = 2 * HEADS_PER_GROUP
    return pl.pallas_call(
        _ssd_kernel,
        grid=(bsz, SSM_GROUPS),
        in_specs=[pl.BlockSpec((1, seq, GROUP_W), lambda b, g: (b, 0, zb + g)),
                  pl.BlockSpec((1, seq, GROUP_W), lambda b, g: (b, 0, xb + g)),
                  pl.BlockSpec((1, seq, SSM_STATE), lambda b, g: (b, 0, bb + g)),
                  pl.BlockSpec((1, seq, SSM_STATE), lambda b, g: (b, 0, cb + g)),
                  pl.BlockSpec((1, 1, seq, n_dir_heads), lambda b, g: (b, g, 0, 0)),
                  pl.BlockSpec((1, 1, seq, n_dir_heads), lambda b, g: (b, g, 0, 0)),
                  pl.BlockSpec((1, 1, n_chunks, n_dir_heads, SSM_CHUNK), lambda b, g: (b, g, 0, 0, 0)),
                  pl.BlockSpec((CONV_WIDTH, GROUP_W), lambda b, g: (0, g)),
                  pl.BlockSpec((CONV_WIDTH, SSM_STATE), lambda b, g: (0, wb + g)),
                  pl.BlockSpec((CONV_WIDTH, SSM_STATE), lambda b, g: (0, wc + g)),
                  pl.BlockSpec((1, GROUP_W), lambda b, g: (0, g)),
                  pl.BlockSpec((1, SSM_STATE), lambda b, g: (0, wb + g)),
                  pl.BlockSpec((1, SSM_STATE), lambda b, g: (0, wc + g)),
                  pl.BlockSpec((1, GROUP_W), lambda b, g: (0, g)),
                  pl.BlockSpec((1, GROUP_W), lambda b, g: (0, g))],
        out_specs=pl.BlockSpec((1, seq, GROUP_W), lambda b, g: (b, 0, g)),
        out_shape=jax.ShapeDtypeStruct((bsz, seq, d_ssm), BF16),
        scratch_shapes=[pltpu.VMEM((seq + 2 * CONV_PAD_ROWS, GROUP_W), F32),
                        pltpu.VMEM((seq, GROUP_W), F32),
                        pltpu.VMEM((seq, SSM_STATE), F32),
                        pltpu.VMEM((seq, SSM_STATE), F32),
                        pltpu.VMEM((seq, GROUP_W), F32),
                        pltpu.VMEM((SSM_STATE, GROUP_W), F32)],
        compiler_params=_cparams(("arbitrary", "arbitrary")),
        name="ssd",
    )(proj, proj, proj, proj, dtg, csg, cstg, conv_w, conv_w, conv_w, conv_b, conv_b, conv_b,
      dskip_e, norm_w)


def _merge_kernel(x_ref, wga_ref, wgs_ref, bga_ref, bgs_ref, ya_ref, ys_ref, wba_ref, wbs_ref, u_ref,
                  wba_b_ref, wbs_b_ref):
    @pl.when(pl.program_id(1) == 0)
    def _():
        wba_b_ref[...] = wba_ref[...].astype(BF16)
        wbs_b_ref[...] = wbs_ref[...].astype(BF16)

    x = x_ref[...]
    nt = (((1,), (1,)), ((), ()))
    ga = _sigmoid(lax.dot_general(x, wga_ref[...], nt, preferred_element_type=F32) + bga_ref[...])
    gs = _sigmoid(lax.dot_general(x, wgs_ref[...], nt, preferred_element_type=F32) + bgs_ref[...])
    ba = jnp.dot(ya_ref[...], wba_b_ref[...], preferred_element_type=F32)
    bs = jnp.dot(ys_ref[...], wbs_b_ref[...], preferred_element_type=F32)
    u_ref[...] = (ga * ba + gs * bs).astype(u_ref.dtype)


def _merge(xb, w_gate, b_gate, y_attn, y_ssm, w_branch, tm, tn):
    m, d = xb.shape
    n = w_branch.shape[1]
    ka, ks = y_attn.shape[1], y_ssm.shape[1]
    nj = n // tn
    return pl.pallas_call(
        _merge_kernel,
        grid=(nj, m // tm),
        in_specs=[pl.BlockSpec((tm, d), lambda j, i: (i, 0)),
                  pl.BlockSpec((tn, d), lambda j, i: (j, 0)),
                  pl.BlockSpec((tn, d), lambda j, i: (nj + j, 0)),
                  pl.BlockSpec((1, tn), lambda j, i: (0, j)),
                  pl.BlockSpec((1, tn), lambda j, i: (0, nj + j)),
                  pl.BlockSpec((tm, ka), lambda j, i: (i, 0)),
                  pl.BlockSpec((tm, ks), lambda j, i: (i, 0)),
                  pl.BlockSpec((ka, tn), lambda j, i: (0, j)),
                  pl.BlockSpec((ks, tn), lambda j, i: (ka // ks, j))],
        out_specs=pl.BlockSpec((tm, tn), lambda j, i: (i, j)),
        out_shape=jax.ShapeDtypeStruct((m, n), BF16),
        scratch_shapes=[pltpu.VMEM((ka, tn), BF16), pltpu.VMEM((ks, tn), BF16)],
        compiler_params=_cparams(("arbitrary", "arbitrary"), VMEM_LIMIT_BIG_TILES),
        name="gated_merge",
    )(xb, w_gate, w_gate, b_gate, b_gate, y_attn, y_ssm, w_branch, w_branch)


def _outproj_kernel(u_ref, w_ref, x_ref, g_ref, b_ref, wr_ref, x1_ref, lg_ref):
    j = pl.program_id(1)
    tn = w_ref.shape[1]
    n_slabs = x1_ref.shape[1] // tn
    part = jnp.dot(u_ref[...], w_ref[...], preferred_element_type=F32)

    for slab in range(n_slabs):
        @pl.when(j == slab)
        def _(slab=slab):
            x1_ref[:, slab * tn:(slab + 1) * tn] = part

    @pl.when(j == n_slabs - 1)
    def _():
        x1 = _layer_norm(ALPHA * x_ref[...] + x1_ref[...], g_ref[...], b_ref[...])
        x1_ref[...] = x1
        lg_ref[...] = jnp.dot(x1.astype(BF16), wr_ref[...], preferred_element_type=F32)


def _outproj_ln(u, w_out, x, g, b, w_router_p, tm, tn):
    m, kdim = u.shape
    d = w_out.shape[1]
    row = pl.BlockSpec((tm, d), lambda i, j: (i, 0))
    par = pl.BlockSpec((1, d), lambda i, j: (0, 0))
    return pl.pallas_call(
        _outproj_kernel,
        grid=(m // tm, d // tn),
        in_specs=[pl.BlockSpec((tm, kdim), lambda i, j: (i, 0)),
                  pl.BlockSpec((kdim, tn), lambda i, j: (0, j)),
                  pl.BlockSpec((tm, d), lambda i, j: (i, 0)),
                  par, par,
                  pl.BlockSpec((d, LANES), lambda i, j: (0, 0), pipeline_mode=pl.Buffered(1))],
        out_specs=[row, pl.BlockSpec((tm, LANES), lambda i, j: (i, 0))],
        out_shape=[jax.ShapeDtypeStruct((m, d), F32), jax.ShapeDtypeStruct((m, LANES), F32)],
        compiler_params=_cparams(("arbitrary", "arbitrary"), VMEM_LIMIT_BIG_TILES),
        name="out_proj_ln1",
    )(u, w_out, x, g, b, w_router_p)


def _routing_kernel(lg_ref, slot_ref, slott_ref, gslot_ref, aff_ref, *, cap):
    seq = lg_ref.shape[1]
    blk = 256
    lg = lg_ref[0]
    valid = lax.broadcasted_iota(jnp.int32, lg.shape, 1) < N_EXPERTS
    lgm = jnp.where(valid, lg, -jnp.inf)
    ex = jnp.exp(lgm - jnp.max(lgm, axis=-1, keepdims=True))
    aff = ex / jnp.sum(ex, axis=-1, keepdims=True)
    aff_t = aff.T[0:N_EXPERTS]
    aff_ref[...] = aff_t
    bits = lax.bitcast_convert_type(aff_t, jnp.int32)

    def count(m):
        return jnp.sum(jnp.where(m, 1.0, 0.0), axis=-1, keepdims=True)

    def search(i, thr):
        cand = thr | jnp.left_shift(jnp.int32(1), 30 - i)
        return jnp.where(count(bits >= cand) >= cap, cand, thr)

    thr = lax.fori_loop(0, 31, search, jnp.zeros((N_EXPERTS, 1), jnp.int32))
    gt = bits > thr
    eq = bits == thr

    r_i = lax.broadcasted_iota(jnp.int32, (blk, blk), 0)
    c_i = lax.broadcasted_iota(jnp.int32, (blk, blk), 1)
    before = jnp.where(r_i < c_i, 1.0, 0.0).astype(BF16)

    def excl_cumsum(m):
        mf = jnp.where(m, 1.0, 0.0)
        carry = jnp.zeros((N_EXPERTS, 1), F32)
        parts = []
        for k in range(seq // blk):
            piece = mf[:, k * blk:(k + 1) * blk]
            parts.append(jnp.dot(piece.astype(BF16), before, preferred_element_type=F32) + carry)
            carry = carry + jnp.sum(piece, axis=-1, keepdims=True)
        return jnp.concatenate(parts, axis=-1)

    need = cap - count(gt)
    sel = gt | (eq & (excl_cumsum(eq) < need))
    slot = jnp.where(sel, excl_cumsum(sel), -1.0)
    slot_ref[0] = slot
    pad = jnp.full((LANES - N_EXPERTS, seq), -1.0, F32)
    slott_ref[0] = jnp.concatenate([slot, pad], axis=0).T

    j_iota = lax.broadcasted_iota(jnp.int32, (cap, seq), 0).astype(F32)

    def gate_of_slot(e, carry):
        hit = slot_ref[0, pl.ds(e, 1), :] == j_iota
        gslot_ref[0, e] = jnp.sum(jnp.where(hit, aff_ref[pl.ds(e, 1), :], 0.0), axis=-1, keepdims=True)
        return carry

    lax.fori_loop(0, N_EXPERTS, gate_of_slot, 0)


def _routing(logits, cap):
    bsz, seq, _ = logits.shape
    return pl.pallas_call(
        functools.partial(_routing_kernel, cap=cap),
        grid=(bsz,),
        in_specs=[pl.BlockSpec((1, seq, LANES), lambda b: (b, 0, 0))],
        out_specs=[pl.BlockSpec((1, N_EXPERTS, seq), lambda b: (b, 0, 0)),
                   pl.BlockSpec((1, seq, LANES), lambda b: (b, 0, 0)),
                   pl.BlockSpec((1, N_EXPERTS, cap, 1), lambda b: (b, 0, 0, 0))],
        out_shape=[jax.ShapeDtypeStruct((bsz, N_EXPERTS, seq), F32),
                   jax.ShapeDtypeStruct((bsz, seq, LANES), F32),
                   jax.ShapeDtypeStruct((bsz, N_EXPERTS, cap, 1), F32)],
        scratch_shapes=[pltpu.VMEM((N_EXPERTS, seq), F32)],
        compiler_params=_cparams(("arbitrary",)),
        name="routing",
    )(logits)


def _gather_kernel(slot_ref, x_ref, o_ref, pick_ref, *, cap):
    seq = x_ref.shape[1]
    td = x_ref.shape[2]

    @pl.when(pl.program_id(1) == 0)
    def _():
        j_iota = lax.broadcasted_iota(jnp.int32, (cap, seq), 0).astype(F32)

        def one_expert(e, carry):
            hit = slot_ref[0, pl.ds(e, 1), :] == j_iota
            pick_ref[pl.ds(pl.multiple_of(e * cap, cap), cap), :] = jnp.where(hit, 1.0, 0.0).astype(BF16)
            return carry

        lax.fori_loop(0, N_EXPERTS, one_expert, 0)

    rows = jnp.dot(pick_ref[...], x_ref[0].astype(BF16), preferred_element_type=F32)
    o_ref[...] = rows.reshape(N_EXPERTS, cap, td).astype(o_ref.dtype)


def _gather(slot, x1, cap, td):
    bsz, seq, d = x1.shape
    return pl.pallas_call(
        functools.partial(_gather_kernel, cap=cap),
        grid=(bsz, d // td),
        in_specs=[pl.BlockSpec((1, N_EXPERTS, seq), lambda b, j: (b, 0, 0)),
                  pl.BlockSpec((1, seq, td), lambda b, j: (b, 0, j))],
        out_specs=pl.BlockSpec((N_EXPERTS, cap, td), lambda b, j: (0, b, j)),
        out_shape=jax.ShapeDtypeStruct((N_EXPERTS, bsz * cap, d), BF16),
        scratch_shapes=[pltpu.VMEM((N_EXPERTS * cap, seq), BF16)],
        compiler_params=_cparams(("arbitrary", "arbitrary")),
        name="moe_gather",
    )(slot, x1)


def _gateup_kernel(xg_ref, wg_ref, wu_ref, h_ref):
    xg = xg_ref[0]
    g = jnp.dot(xg, wg_ref[0].astype(BF16), preferred_element_type=F32)
    u = jnp.dot(xg, wu_ref[0].astype(BF16), preferred_element_type=F32)
    h_ref[0] = (_silu(g) * u).astype(h_ref.dtype)


def _gateup(xg, w_gate_e, w_up_e, tf):
    n_e, rows, d = xg.shape
    ff = w_gate_e.shape[2]
    wspec = pl.BlockSpec((1, d, tf), lambda e, f: (e, 0, f))
    return pl.pallas_call(
        _gateup_kernel,
        grid=(n_e, ff // tf),
        in_specs=[pl.BlockSpec((1, rows, d), lambda e, f: (e, 0, 0)), wspec, wspec],
        out_specs=pl.BlockSpec((1, rows, tf), lambda e, f: (e, 0, f)),
        out_shape=jax.ShapeDtypeStruct((n_e, rows, ff), BF16),
        compiler_params=_cparams(("arbitrary", "arbitrary")),
        name="moe_gate_up",
    )(xg, w_gate_e, w_up_e)


def _down_kernel(h_ref, wd_ref, gs_ref, y_ref):
    y = jnp.dot(h_ref[0], wd_ref[0].astype(BF16), preferred_element_type=F32)
    y_ref[0] = (y * gs_ref[0]).astype(y_ref.dtype)


def _down(h, w_down_e, gslot, td):
    n_e, rows, ff = h.shape
    d = w_down_e.shape[2]
    return pl.pallas_call(
        _down_kernel,
        grid=(n_e, d // td),
        in_specs=[pl.BlockSpec((1, rows, ff), lambda e, j: (e, 0, 0)),
                  pl.BlockSpec((1, ff, td), lambda e, j: (e, 0, j)),
                  pl.BlockSpec((1, rows, 1), lambda e, j: (e, 0, 0))],
        out_specs=pl.BlockSpec((1, rows, td), lambda e, j: (e, 0, j)),
        out_shape=jax.ShapeDtypeStruct((n_e, rows, d), BF16),
        compiler_params=_cparams(("arbitrary", "arbitrary")),
        name="moe_down",
    )(h, w_down_e, gslot)


def _scatter_kernel(slott_ref, yg_ref, x1_ref, g_ref, b_ref, o_ref, put_ref, *, cap):
    dj = pl.program_id(2)
    ts = o_ref.shape[1]
    td = yg_ref.shape[2]
    n_slabs = o_ref.shape[2] // td

    @pl.when(dj == 0)
    def _():
        st = slott_ref[0]
        lane = lax.broadcasted_iota(jnp.int32, st.shape, 1)
        j_iota = lax.broadcasted_iota(jnp.int32, (ts, cap), 1).astype(F32)
        for e in range(N_EXPERTS):
            col = jnp.sum(jnp.where(lane == e, st, 0.0), axis=-1, keepdims=True)
            put_ref[:, e * cap:(e + 1) * cap] = jnp.where(col == j_iota, 1.0, 0.0).astype(BF16)

    part = jnp.dot(put_ref[...], yg_ref[...].reshape(N_EXPERTS * cap, td), preferred_element_type=F32)

    for slab in range(n_slabs):
        @pl.when(dj == slab)
        def _(slab=slab):
            o_ref[0, :, slab * td:(slab + 1) * td] = part

    @pl.when(dj == n_slabs - 1)
    def _():
        o_ref[0] = _layer_norm(ALPHA * x1_ref[0] + o_ref[0], g_ref[...], b_ref[...])


def _scatter_ln(slot_t, yg, x1, g, b, cap, ts, td):
    bsz, seq, d = x1.shape
    par = pl.BlockSpec((1, d), lambda bi, i, j: (0, 0))
    return pl.pallas_call(
        functools.partial(_scatter_kernel, cap=cap),
        grid=(bsz, seq // ts, d // td),
        in_specs=[pl.BlockSpec((1, ts, LANES), lambda bi, i, j: (bi, i, 0)),
                  pl.BlockSpec((N_EXPERTS, cap, td), lambda bi, i, j: (0, bi, j)),
                  pl.BlockSpec((1, ts, d), lambda bi, i, j: (bi, i, 0)),
                  par, par],
        out_specs=pl.BlockSpec((1, ts, d), lambda bi, i, j: (bi, i, 0)),
        out_shape=jax.ShapeDtypeStruct((bsz, seq, d), F32),
        scratch_shapes=[pltpu.VMEM((ts, N_EXPERTS * cap), BF16)],
        compiler_params=_cparams(("arbitrary", "arbitrary", "arbitrary"), VMEM_LIMIT_BIG_TILES),
        name="moe_scatter_ln2",
    )(slot_t, yg, x1, g, b)


def _group_heads(t):
    bsz, seq, _ = t.shape
    t = t[:, :, :2 * SSM_HEADS].reshape(bsz, seq, 2, SSM_GROUPS, HEADS_PER_GROUP)
    return jnp.transpose(t, (0, 3, 1, 2, 4)).reshape(bsz, SSM_GROUPS, seq, 2 * HEADS_PER_GROUP)


def _layer(x, w_in, b_gate, lq1, lk1, lq2, lk2, subln_w, conv_w, conv_b, dtb_f, dtb_b, alog_f, alog_b,
           d_skip, ssm_norm_w, w_branch, w_out, ln1_g, ln1_b, w_router, w_gate_e, w_up_e, w_down_e,
           ln2_g, ln2_b, layer_idx):
    bsz, seq, d = x.shape
    m = bsz * seq
    d_qk = ATTN_HEADS * 2 * ATTN_HEAD_DIM
    d_v = ATTN_HEADS * ATTN_V_DIM
    d_ssm = SSM_HEADS * SSM_HEAD_DIM
    d_conv = d_ssm + 2 * SSM_GROUPS * SSM_STATE
    n_main = 2 * d_qk + d_v + d_ssm + d_conv
    n_dt = 2 * SSM_HEADS
    lambda_init = 0.8 - 0.6 * math.exp(-0.3 * layer_idx)
    cap = CAPACITY_FACTOR * seq // N_EXPERTS
    row = lambda v: v.reshape(1, -1)

    w_in_t = w_in.T
    xb, dt_raw = _xcast_dt(x.reshape(m, d), w_in_t, n_main // LANES)
    dt_raw = dt_raw.reshape(bsz, seq, LANES)

    tn_main = 768
    proj, w_gate_t = _matmul_wt(xb, w_in_t, n_main // tn_main, n_main + n_dt, 2 * d, F32, 1024, tn_main,
                                "in_proj")
    proj = proj.reshape(bsz, seq, n_main)

    lane_pad = lambda a, bvec: jnp.pad(jnp.concatenate([a, bvec]), (0, LANES - n_dt)).reshape(1, LANES)
    dt, cs = _dtprep(dt_raw, lane_pad(dtb_f, dtb_b), lane_pad(alog_f, alog_b))
    dtg, csg = _group_heads(dt), _group_heads(cs)
    n_chunks = seq // SSM_CHUNK
    cstg = jnp.transpose(csg.reshape(bsz, SSM_GROUPS, n_chunks, SSM_CHUNK, 2 * HEADS_PER_GROUP),
                         (0, 1, 2, 4, 3))

    y_attn, w_out_b = _attention(proj, row(lq1), row(lk1), row(lq2), row(lk2), row(subln_w), w_out,
                                 lambda_init, 256)
    y_ssm = _ssd(proj, dtg, csg, cstg, conv_w, row(conv_b), row(jnp.repeat(d_skip, SSM_HEAD_DIM)),
                 row(ssm_norm_w), 2 * d_qk + d_v, 2 * d_qk + d_v + d_ssm)

    u = _merge(xb, w_gate_t, row(b_gate), y_attn.reshape(m, d_v), y_ssm.reshape(m, d_ssm),
               w_branch, 512, 512)
    w_router_p = jnp.pad(w_router, ((0, 0), (0, LANES - N_EXPERTS))).astype(BF16)
    x1, logits = _outproj_ln(u, w_out_b, x.reshape(m, d), row(ln1_g), row(ln1_b),
                             w_router_p, 512, 512)
    x1 = x1.reshape(bsz, seq, d)

    slot, slot_t, gslot = _routing(logits.reshape(bsz, seq, LANES), cap)
    xg = _gather(slot, x1, cap, 512)
    h = _gateup(xg, w_gate_e, w_up_e, 256)
    gslot_e = jnp.transpose(gslot, (1, 0, 2, 3)).reshape(N_EXPERTS, bsz * cap, 1)
    yg = _down(h, w_down_e, gslot_e, 1024)
    return _scatter_ln(slot_t, yg, x1, row(ln2_g), row(ln2_b), cap, 512, 512)


def kernel(x, w_in, b_gate, lambda_q1, lambda_k1, lambda_q2, lambda_k2, attn_subln_w, conv_w, conv_b,
           dt_bias_fwd, dt_bias_bwd, a_log_fwd, a_log_bwd, d_skip, ssm_norm_w, w_branch, w_out,
           ln1_g, ln1_b, w_router, w_gate_e, w_up_e, w_down_e, ln2_g, ln2_b):
    for l in range(w_in.shape[0]):
        x = _layer(x, w_in[l], b_gate[l], lambda_q1[l], lambda_k1[l], lambda_q2[l], lambda_k2[l],
                   attn_subln_w[l], conv_w[l], conv_b[l], dt_bias_fwd[l], dt_bias_bwd[l],
                   a_log_fwd[l], a_log_bwd[l], d_skip[l], ssm_norm_w[l], w_branch[l], w_out[l],
                   ln1_g[l], ln1_b[l], w_router[l], w_gate_e[l], w_up_e[l], w_down_e[l],
                   ln2_g[l], ln2_b[l], l)
    return x
```
